```python
import math
import jax
import jax.numpy as jnp
from jax import lax
import numpy as np

D_MODEL = 1024
BATCH = 16
SEQ = 2048
DEPTH = 2
DEC_BATCH = 128
DEC_SEQ = 1
PAST_LEN = 8192
PAGE_SIZE = 128

N_AB_LAYERS = (DEPTH + 1) // 2
N_C_LAYERS = DEPTH // 2

NSA_HEADS = 8
NSA_KV_HEADS = 2
NSA_HEAD_DIM = 64
NSA_GROUP = NSA_HEADS // NSA_KV_HEADS
CMP_BLOCK = 32
CMP_STRIDE = 16
CMP_HIDDEN = 64
SEL_BLOCK = 64
SEL_TOPN = 8
SEL_LOCAL = 2
NSA_WINDOW = 512
N_BRANCH = 3
FORCE_SCORE = 1.0e6

RET_HEADS = 4
RET_DK = 128
RET_DV = 128
RET_CHUNK = 128

MLA_HEADS = 16
MLA_Q_LORA = 384
MLA_KV_LORA = 256
MLA_NOPE = 64
MLA_ROPE = 32
MLA_V = 64
MLA_CACHE_W = MLA_KV_LORA + MLA_ROPE
ROPE_THETA = 10000.0

N_EXPERTS = 16
N_GROUPS = 4
EXPERTS_PER_GROUP = N_EXPERTS // N_GROUPS
TOPK_GROUPS = 1
TOPK = 2
GROUP_SCORE_TOPK = 2
D_EXPERT = 512

NSA_Q_W = NSA_HEADS * NSA_HEAD_DIM
NSA_KV_W = 6 * NSA_KV_HEADS * NSA_HEAD_DIM
NSA_G_W = NSA_HEADS * N_BRANCH
RET_QK_W = RET_HEADS * RET_DK
RET_V_W = RET_HEADS * RET_DV
AB_IN = NSA_Q_W + NSA_KV_W + NSA_G_W + 2 * RET_QK_W + 2 * RET_V_W
AB_MIX = NSA_Q_W + RET_V_W
C_IN = MLA_Q_LORA + MLA_KV_LORA + MLA_ROPE
C_MIX = MLA_HEADS * MLA_V

Q_BLOCK = 128
LN_EPS = 1e-5
RMS_EPS = 1e-6
DEEPNORM_ALPHA = (2 * DEPTH) ** 0.25
DEEPNORM_BETA = (8 * DEPTH) ** -0.25

kernel_name = 'hybrid_nsa_retention_mla_moe_decode_step'

F32 = jnp.float32


def _layer_norm(x, g, b):
    xf = x.astype(F32)
    mu = jnp.mean(xf, -1, keepdims=True)
    var = jnp.mean(jnp.square(xf - mu), -1, keepdims=True)
    return ((xf - mu) * lax.rsqrt(var + LN_EPS) * g.astype(F32) + b.astype(F32)).astype(x.dtype)


def _head_norm(x):
    xf = x.astype(F32)
    mu = jnp.mean(xf, -1, keepdims=True)
    var = jnp.mean(jnp.square(xf - mu), -1, keepdims=True)
    return ((xf - mu) * lax.rsqrt(var + LN_EPS)).astype(x.dtype)


def _rms_norm(x, g):
    xf = x.astype(F32)
    return (xf * lax.rsqrt(jnp.mean(xf * xf, -1, keepdims=True) + RMS_EPS) * g.astype(F32)).astype(x.dtype)


def _rope(x, pos):
    half = x.shape[-1] // 2
    inv = ROPE_THETA ** (-jnp.arange(half, dtype=F32) / half)
    ang = pos.astype(F32)[:, None] * inv[None, :]
    cos = jnp.cos(ang)[:, None, :]
    sin = jnp.sin(ang)[:, None, :]
    xf = x.astype(F32)
    x1, x2 = xf[..., :half], xf[..., half:]
    return jnp.concatenate([x1 * cos - x2 * sin, x2 * cos + x1 * sin], -1).astype(x.dtype)


def _masked_softmax(s, mask):
    s = jnp.where(mask, s.astype(F32), -jnp.inf)
    m = jnp.max(s, -1, keepdims=True)
    m = jnp.where(jnp.isfinite(m), m, 0.0)
    e = jnp.exp(s - m)
    return e / jnp.maximum(jnp.sum(e, -1, keepdims=True), jnp.finfo(F32).tiny)


def _query_blocks(fn, T, arrays):
    qb = min(Q_BLOCK, T)
    nb = -(-T // qb)
    pad = nb * qb - T

    def split(a):
        a = jnp.pad(a, [(0, 0), (0, pad)] + [(0, 0)] * (a.ndim - 2))
        return jnp.moveaxis(a.reshape(a.shape[0], nb, qb, *a.shape[2:]), 1, 0)

    starts = jnp.arange(nb, dtype=jnp.int32) * qb
    out = lax.map(lambda args: fn(args[0], *args[1]), (starts, tuple(split(a) for a in arrays)))
    out = jnp.moveaxis(out, 0, 1)
    return out.reshape(out.shape[0], nb * qb, *out.shape[3:])[:, :T]


def _gather_pages(pool, page_table):
    g = pool[page_table]
    return g.reshape(g.shape[0], g.shape[1] * g.shape[2], *g.shape[3:])


def _compress(kv, w1, w2, pe):
    B, L, G, hd = kv.shape
    r = CMP_BLOCK // CMP_STRIDE
    n_ch = L // CMP_STRIDE
    n_cmp = n_ch - r + 1
    ch = kv[:, :n_ch * CMP_STRIDE].reshape(B, n_ch, CMP_STRIDE, G, hd)
    ch = ch.transpose(0, 1, 3, 2, 4).reshape(B, n_ch, G, CMP_STRIDE * hd)
    w1r = w1.reshape(r, CMP_STRIDE * hd, CMP_HIDDEN)
    hdn = jnp.einsum('f,fe->e', pe.reshape(-1), w1)
    for j in range(r):
        hdn = hdn + jnp.einsum('bngf,fe->bnge', ch[:, j:j + n_cmp], w1r[j])
    return jnp.einsum('bnge,ed->bngd', jax.nn.gelu(hdn), w2)


def _nsa_attend(q, gates, kv_full, win_pad, w_cmp1, w_cmp2, cmp_pe, P):
    B, T, H, hd = q.shape
    G, hpg = NSA_KV_HEADS, NSA_GROUP
    L = kv_full.shape[1]
    scale = hd ** -0.5
    k_cmp = _compress(kv_full[:, :, 0], w_cmp1[0], w_cmp2[0], cmp_pe[0])
    v_cmp = _compress(kv_full[:, :, 1], w_cmp1[1], w_cmp2[1], cmp_pe[1])
    n_cmp = k_cmp.shape[1]
    cmp_last = jnp.arange(n_cmp) * CMP_STRIDE + CMP_BLOCK - 1
    n_slc = -(-L // SEL_BLOCK)
    n_top = min(SEL_TOPN, n_slc)
    ci = jnp.arange(n_cmp)[:, None] * CMP_STRIDE
    sj = jnp.arange(n_slc)[None, :] * SEL_BLOCK
    sel_map = (jnp.clip(jnp.minimum(ci + CMP_BLOCK, sj + SEL_BLOCK) - jnp.maximum(ci, sj), 0, None) / CMP_STRIDE).astype(F32)
    slc = jnp.pad(kv_full[:, :, 2:4], [(0, 0), (0, n_slc * SEL_BLOCK - L), (0, 0), (0, 0), (0, 0)])
    slc = slc.reshape(B, n_slc, SEL_BLOCK, 2, G, hd).transpose(3, 0, 4, 1, 2, 5)
    blk = jnp.arange(n_slc)
    bi = jnp.arange(B)[:, None, None, None]
    gi = jnp.arange(G)[None, :, None, None]

    def block(i0, qblk, gblk):
        nq = qblk.shape[1]
        qpos = P + i0 + jnp.arange(nq)
        qg = qblk.reshape(B, nq, G, hpg, hd)
        gg = gblk.reshape(B, nq, G, hpg, N_BRANCH)
        s = jnp.einsum('bqghd,bngd->bghqn', qg, k_cmp) * scale
        p_cmp = _masked_softmax(s, cmp_last[None, :] <= qpos[:, None])
        o_cmp = jnp.einsum('bghqn,bngd->bqghd', p_cmp.astype(q.dtype), v_cmp)
        imp = jnp.einsum('bghqn,ns->bgqs', p_cmp, sel_map)
        cur = qpos // SEL_BLOCK
        forced = (blk[None, :] == 0) | ((blk[None, :] <= cur[:, None]) & (blk[None, :] > cur[:, None] - SEL_LOCAL))
        imp = jnp.where(forced, FORCE_SCORE, imp)
        imp = jnp.where(blk[None, :] * SEL_BLOCK <= qpos[:, None], imp, -jnp.inf)
        _, idx = lax.top_k(imp, n_top)
        ks = slc[0][bi, gi, idx].reshape(B, G, nq, n_top * SEL_BLOCK, hd)
        vs = slc[1][bi, gi, idx].reshape(B, G, nq, n_top * SEL_BLOCK, hd)
        kpos = (idx[..., None] * SEL_BLOCK + jnp.arange(SEL_BLOCK)).reshape(B, G, nq, n_top * SEL_BLOCK)
        s = jnp.einsum('bqghd,bgqkd->bghqk', qg, ks) * scale
        p = _masked_softmax(s, (kpos <= qpos[None, None, :, None])[:, :, None])
        o_slc = jnp.einsum('bghqk,bgqkd->bqghd', p.astype(q.dtype), vs)
        wkv = lax.dynamic_slice_in_dim(win_pad, i0, nq + NSA_WINDOW, axis=1)
        kp = P - NSA_WINDOW + i0 + jnp.arange(nq + NSA_WINDOW)
        wmask = (kp[None, :] >= 0) & (kp[None, :] <= qpos[:, None]) & (kp[None, :] > qpos[:, None] - NSA_WINDOW)
        s = jnp.einsum('bqghd,bkgd->bghqk', qg, wkv[:, :, 0]) * scale
        p = _masked_softmax(s, wmask)
        o_win = jnp.einsum('bghqk,bkgd->bqghd', p.astype(q.dtype), wkv[:, :, 1])
        o = gg[..., 0:1] * o_cmp + gg[..., 1:2] * o_slc + gg[..., 2:3] * o_win
        return o.reshape(B, nq, H * hd)

    return _query_blocks(block, T, (q, gates))


def _retention(q, k, v, S0):
    B, T, H, dk = q.shape
    dv = v.shape[-1]
    C = math.gcd(T, RET_CHUNK)
    n = T // C
    log_g = jnp.log1p(-jnp.exp2(-5.0 - jnp.arange(H, dtype=F32)))
    t = jnp.arange(C, dtype=F32)
    diff = t[:, None] - t[None, :]
    D = jnp.where(diff >= 0, jnp.exp(log_g[:, None, None] * jnp.maximum(diff, 0.0)), 0.0)
    xi = jnp.exp(log_g[:, None] * (t + 1.0))[..., None]
    zeta = jnp.exp(log_g[:, None] * (C - 1.0 - t))[..., None]
    g_chunk = jnp.exp(log_g * C)[:, None, None]

    def chunks(a):
        return a.astype(F32).reshape(B, n, C, H, a.shape[-1]).transpose(1, 0, 3, 2, 4)

    def step(S, inp):
        qc, kc, vc = inp
        inner = jnp.einsum('bhnd,bhmd->bhnm', qc, kc) * D
        o = jnp.einsum('bhnm,bhmv->bhnv', inner, vc) + jnp.einsum('bhnd,bhdv->bhnv', qc * xi, S)
        S = g_chunk * S + jnp.einsum('bhmd,bhmv->bhdv', kc * zeta, vc)
        return S, o

    S, o = lax.scan(step, S0.astype(F32), (chunks(q), chunks(k), chunks(v)))
    o = o.transpose(1, 0, 3, 2, 4).reshape(B, T, H, dv)
    return o.astype(q.dtype), S.astype(S0.dtype)


def _mixer_ab(x, P, past_kv, win_buf, ret_state, w_in, w_out, w_cmp1, w_cmp2, cmp_pe):
    B, T, _ = x.shape
    G, hd = NSA_KV_HEADS, NSA_HEAD_DIM
    sizes = [NSA_Q_W, NSA_KV_W, NSA_G_W, RET_QK_W, RET_QK_W, RET_V_W, RET_V_W]
    offs = np.cumsum(sizes)[:-1].tolist()
    h = jnp.einsum('btd,de->bte', x, w_in)
    q_nsa, kv_nsa, g_nsa, q_ret, k_ret, v_ret, g_ret = jnp.split(h, offs, axis=-1)
    q_nsa = q_nsa.reshape(B, T, NSA_HEADS, hd)
    kv_nsa = kv_nsa.reshape(B, T, 3, 2, G, hd)
    new_kv = kv_nsa[:, :, :2].reshape(B, T, 4, G, hd)
    new_win = kv_nsa[:, :, 2]
    kv_full = jnp.concatenate([past_kv, new_kv], axis=1)
    win_all = jnp.concatenate([win_buf, new_win], axis=1)
    n_keep = min(NSA_WINDOW, P + T)
    win_state = win_all[:, win_all.shape[1] - n_keep:]
    win_pad = jnp.pad(win_all, [(0, 0), (NSA_WINDOW - win_buf.shape[1], Q_BLOCK), (0, 0), (0, 0), (0, 0)])
    gates = jax.nn.sigmoid(g_nsa.reshape(B, T, NSA_HEADS, N_BRANCH))
    o_nsa = _nsa_attend(q_nsa, gates, kv_full, win_pad, w_cmp1, w_cmp2, cmp_pe, P)
    pos = P + jnp.arange(T)
    q_ret = _rope(q_ret.reshape(B, T, RET_HEADS, RET_DK), pos)
    k_ret = _rope(k_ret.reshape(B, T, RET_HEADS, RET_DK), pos) * (RET_DK ** -0.5)
    v_ret = v_ret.reshape(B, T, RET_HEADS, RET_DV)
    o_ret, S = _retention(q_ret, k_ret, v_ret, ret_state)
    o_ret = _head_norm(o_ret) * jax.nn.silu(g_ret.reshape(B, T, RET_HEADS, RET_DV))
    mixed = jnp.concatenate([o_nsa, o_ret.reshape(B, T, RET_V_W)], axis=-1)
    return jnp.einsum('bte,ed->btd', mixed, w_out), new_kv, win_state, S


def _mixer_c(x, P, past_rows, w_in, q_norm_g, kv_norm_g, w_qb, w_kb, w_vb, w_out):
    B, T, _ = x.shape
    h = jnp.einsum('btd,de->bte', x, w_in)
    c_q, c_kv, k_r = jnp.split(h, [MLA_Q_LORA, MLA_Q_LORA + MLA_KV_LORA], axis=-1)
    pos = P + jnp.arange(T)
    q = jnp.einsum('btc,chd->bthd', _rms_norm(c_q, q_norm_g), w_qb)
    q_nope = q[..., :MLA_NOPE]
    q_rope = _rope(q[..., MLA_NOPE:], pos)
    new_rows = jnp.concatenate([_rms_norm(c_kv, kv_norm_g), _rope(k_r[:, :, None], pos)[:, :, 0]], axis=-1)
    rows = jnp.concatenate([past_rows, new_rows], axis=1)
    ckv, kr = rows[..., :MLA_KV_LORA], rows[..., MLA_KV_LORA:]
    kpos = jnp.arange(rows.shape[1])
    scale = (MLA_NOPE + MLA_ROPE) ** -0.5

    def block(i0, qn, qr):
        nq = qn.shape[1]
        qpos = P + i0 + jnp.arange(nq)
        q_lat = jnp.einsum('bqhd,chd->bqhc', qn, w_kb)
        s = (jnp.einsum('bqhc,bkc->bhqk', q_lat, ckv) + jnp.einsum('bqhr,bkr->bhqk', qr, kr)) * scale
        p = _masked_softmax(s, kpos[None, :] <= qpos[:, None])
        o_lat = jnp.einsum('bhqk,bkc->bqhc', p.astype(ckv.dtype), ckv)
        return jnp.einsum('bqhc,chd->bqhd', o_lat, w_vb).reshape(B, nq, C_MIX)

    o = _query_blocks(block, T, (q_nope, q_rope))
    return jnp.einsum('bte,ed->btd', o, w_out), new_rows


def _moe(x, w_router, router_bias, w_gate, w_up, w_down):
    B, T, D = x.shape
    xt = x.reshape(B * T, D)
    s = jax.nn.sigmoid(jnp.einsum('nd,de->ne', xt, w_router).astype(F32))
    s_sel = s + router_bias.astype(F32)
    grp = s_sel.reshape(-1, N_GROUPS, EXPERTS_PER_GROUP)
    gscore = jnp.sum(lax.top_k(grp, GROUP_SCORE_TOPK)[0], axis=-1)
    _, gidx = lax.top_k(gscore, TOPK_GROUPS)
    gmask = jnp.any(jnp.arange(N_GROUPS)[None, None, :] == gidx[:, :, None], axis=1)
    emask = jnp.repeat(gmask, EXPERTS_PER_GROUP, axis=1)
    _, eidx = lax.top_k(jnp.where(emask, s_sel, -jnp.inf), TOPK)
    w = jnp.take_along_axis(s, eidx, axis=1)
    w = w / jnp.sum(w, axis=-1, keepdims=True)
    gates = jnp.sum(jax.nn.one_hot(eidx, N_EXPERTS, dtype=F32) * w[..., None], axis=1).astype(x.dtype)
    y = jnp.zeros_like(xt)
    for e in range(N_EXPERTS):
        hdn = jax.nn.silu(xt @ w_gate[e]) * (xt @ w_up[e])
        y = y + gates[:, e:e + 1] * (hdn @ w_down[e])
    return y.reshape(B, T, D)


def setup_inputs(seed: int = 0) -> dict:
    key = jax.random.key(seed)
    keys = iter(jax.random.split(key, 40))

    def nrm(shape, scale):
        return jax.random.normal(next(keys), shape, jnp.float32) * scale

    n_pages = PAST_LEN // PAGE_SIZE
    n_phys = (5 * DEC_BATCH * n_pages) // 4
    wb = min(NSA_WINDOW, PAST_LEN)
    G, hd = NSA_KV_HEADS, NSA_HEAD_DIM
    inp = {}
    inp['x_prompt'] = nrm((BATCH, SEQ, D_MODEL), 1.0)
    inp['x_sample'] = nrm((DEC_BATCH, DEC_SEQ, D_MODEL), 1.0)
    inp['cache_nsa_kv'] = nrm((N_AB_LAYERS, n_phys, PAGE_SIZE, 4, G, hd), 1.0)
    inp['state_nsa_win'] = nrm((N_AB_LAYERS, DEC_BATCH, wb, 2, G, hd), 1.0)
    inp['state_ret'] = nrm((N_AB_LAYERS, DEC_BATCH, RET_HEADS, RET_DK, RET_DV), 0.5)
    inp['cache_mla'] = nrm((N_C_LAYERS, n_phys, PAGE_SIZE, MLA_CACHE_W), 1.0)
    perm = jax.random.permutation(next(keys), n_phys)
    inp['page_table'] = perm[:DEC_BATCH * n_pages].reshape(DEC_BATCH, n_pages).astype(jnp.int32)
    inp['w_in_ab'] = nrm((N_AB_LAYERS, D_MODEL, AB_IN), D_MODEL ** -0.5)
    inp['w_out_ab'] = nrm((N_AB_LAYERS, AB_MIX, D_MODEL), DEEPNORM_BETA * AB_MIX ** -0.5)
    inp['w_cmp1'] = nrm((N_AB_LAYERS, 2, CMP_BLOCK * hd, CMP_HIDDEN), (CMP_BLOCK * hd) ** -0.5)
    inp['w_cmp2'] = nrm((N_AB_LAYERS, 2, CMP_HIDDEN, hd), CMP_HIDDEN ** -0.5)
    inp['cmp_pe'] = nrm((N_AB_LAYERS, 2, CMP_BLOCK, hd), 0.1)
    inp['w_in_mla'] = nrm((N_C_LAYERS, D_MODEL, C_IN), D_MODEL ** -0.5)
    inp['mla_q_norm'] = 1.0 + nrm((N_C_LAYERS, MLA_Q_LORA), 0.05)
    inp['mla_kv_norm'] = 1.0 + nrm((N_C_LAYERS, MLA_KV_LORA), 0.05)
    inp['w_q_up'] = nrm((N_C_LAYERS, MLA_Q_LORA, MLA_HEADS, MLA_NOPE + MLA_ROPE), MLA_Q_LORA ** -0.5)
    inp['w_k_up'] = nrm((N_C_LAYERS, MLA_KV_LORA, MLA_HEADS, MLA_NOPE), MLA_KV_LORA ** -0.5)
    inp['w_v_up'] = nrm((N_C_LAYERS, MLA_KV_LORA, MLA_HEADS, MLA_V), MLA_KV_LORA ** -0.5)
    inp['w_out_mla'] = nrm((N_C_LAYERS, C_MIX, D_MODEL), DEEPNORM_BETA * C_MIX ** -0.5)
    inp['ln_mix_g'] = 1.0 + nrm((DEPTH, D_MODEL), 0.05)
    inp['ln_mix_b'] = nrm((DEPTH, D_MODEL), 0.02)
    inp['ln_ffn_g'] = 1.0 + nrm((DEPTH, D_MODEL), 0.05)
    inp['ln_ffn_b'] = nrm((DEPTH, D_MODEL), 0.02)
    inp['w_router'] = nrm((D_MODEL, N_EXPERTS), D_MODEL ** -0.5)
    inp['router_bias'] = nrm((N_EXPERTS,), 0.01)
    inp['w_exp_gate'] = nrm((DEPTH, N_EXPERTS, D_MODEL, D_EXPERT), D_MODEL ** -0.5)
    inp['w_exp_up'] = nrm((DEPTH, N_EXPERTS, D_MODEL, D_EXPERT), D_MODEL ** -0.5)
    inp['w_exp_down'] = nrm((DEPTH, N_EXPERTS, D_EXPERT, D_MODEL), DEEPNORM_BETA * D_EXPERT ** -0.5)
    return inp


def reference(x_prompt, x_sample, cache_nsa_kv, state_nsa_win, state_ret, cache_mla, page_table,
              w_in_ab, w_out_ab, w_cmp1, w_cmp2, cmp_pe,
              w_in_mla, mla_q_norm, mla_kv_norm, w_q_up, w_k_up, w_v_up, w_out_mla,
              ln_mix_g, ln_mix_b, ln_ffn_g, ln_ffn_b,
              w_router, router_bias, w_exp_gate, w_exp_up, w_exp_down):
    Bp = x_prompt.shape[0]
    P = page_table.shape[1] * PAGE_SIZE
    G, hd = NSA_KV_HEADS, NSA_HEAD_DIM
    dt = x_prompt.dtype
    xp, xs = x_prompt, x_sample
    kv_p, kv_s, win_p, win_s, ret_p, ret_s, mla_p, mla_s = [], [], [], [], [], [], [], []
    for layer in range(DEPTH):
        i = layer // 2
        if layer % 2 == 0:
            wts = (w_in_ab[i], w_out_ab[i], w_cmp1[i], w_cmp2[i], cmp_pe[i])
            mp, a1, a2, a3 = _mixer_ab(xp, 0, jnp.zeros((Bp, 0, 4, G, hd), dt), jnp.zeros((Bp, 0, 2, G, hd), dt),
                                       jnp.zeros((Bp, RET_HEADS, RET_DK, RET_DV), dt), *wts)
            ms, b1, b2, b3 = _mixer_ab(xs, P, _gather_pages(cache_nsa_kv[i], page_table), state_nsa_win[i],
                                       state_ret[i], *wts)
            kv_p.append(a1); win_p.append(a2); ret_p.append(a3)
            kv_s.append(b1); win_s.append(b2); ret_s.append(b3)
        else:
            wts = (w_in_mla[i], mla_q_norm[i], mla_kv_norm[i], w_q_up[i], w_k_up[i], w_v_up[i], w_out_mla[i])
            mp, a1 = _mixer_c(xp, 0, jnp.zeros((Bp, 0, MLA_CACHE_W), dt), *wts)
            ms, b1 = _mixer_c(xs, P, _gather_pages(cache_mla[i], page_table), *wts)
            mla_p.append(a1); mla_s.append(b1)
        xp = _layer_norm(DEEPNORM_ALPHA * xp + mp, ln_mix_g[layer], ln_mix_b[layer])
        xs = _layer_norm(DEEPNORM_ALPHA * xs + ms, ln_mix_g[layer], ln_mix_b[layer])
        moe_w = (w_router, router_bias, w_exp_gate[layer], w_exp_up[layer], w_exp_down[layer])
        xp = _layer_norm(DEEPNORM_ALPHA * xp + _moe(xp, *moe_w), ln_ffn_g[layer], ln_ffn_b[layer])
        xs = _layer_norm(DEEPNORM_ALPHA * xs + _moe(xs, *moe_w), ln_ffn_g[layer], ln_ffn_b[layer])
    new_nsa_kv_prompt = jnp.stack(kv_p)
    new_nsa_kv_sample = jnp.stack(kv_s)
    new_nsa_win_prompt = jnp.stack(win_p)
    new_nsa_win_sample = jnp.stack(win_s)
    new_ret_prompt = jnp.stack(ret_p)
    new_ret_sample = jnp.stack(ret_s)
    new_mla_prompt = jnp.stack(mla_p)
    new_mla_sample = jnp.stack(mla_s)
    return (xp, xs, new_nsa_kv_prompt, new_nsa_kv_sample, new_nsa_win_prompt, new_nsa_win_sample,
            new_ret_prompt, new_ret_sample, new_mla_prompt, new_mla_sample)
```

```python
import functools
import math

import numpy as np
import jax
import jax.numpy as jnp
from jax import lax
from jax.experimental import pallas as pl
from jax.experimental.pallas import tpu as pltpu

F32 = jnp.float32
BF16 = jnp.bfloat16
I32 = jnp.int32

PAGE = 128
NSA_H, NSA_G, NSA_HD = 8, 2, 64
NSA_HPG = NSA_H // NSA_G
CMP_BLOCK, CMP_STRIDE, CMP_HIDDEN = 32, 16, 64
SEL_BLOCK, SEL_TOPN, SEL_LOCAL = 64, 8, 2
NSA_WINDOW = 512
FORCE_SCORE = 1.0e6
RET_H, RET_DK, RET_DV, RET_CHUNK = 4, 128, 128, 128
MLA_H, MLA_Q_LORA, MLA_KV_LORA, MLA_NOPE, MLA_ROPE, MLA_V = 16, 384, 256, 64, 32, 64
ROPE_THETA = 10000.0
N_EXPERTS, N_GROUPS, EPG, D_EXPERT = 16, 4, 4, 512
N_PAIRS = EPG * (EPG - 1) // 2
N_CLASSES = N_GROUPS * N_PAIRS
Q_BLOCK = 128
LN_EPS = 1e-5
RMS_EPS = 1e-6
DEPTH = 2
ALPHA = (2 * DEPTH) ** 0.25

LANES = 128
MXU_N = 256
VMEM_LIMIT = 48 * 1024 * 1024

TOK_TILE = 512
MOE_TILE = 256
CLS_PAD = 32

AB_Q, AB_KV, AB_QR, AB_KR, AB_VR, AB_GR, AB_GATE = 0, 512, 1280, 1792, 2304, 2816, 3328
AB_W = 3584


def _cparams(n_axes):
    return pltpu.CompilerParams(dimension_semantics=("arbitrary",) * n_axes, vmem_limit_bytes=VMEM_LIMIT)


def _mm_kernel(x_ref, w_ref, o_ref):
    xb = x_ref[...].astype(BF16)
    for c in range(0, o_ref.shape[1], 2 * MXU_N):
        o_ref[:, c:c + 2 * MXU_N] = jnp.dot(xb, w_ref[:, c:c + 2 * MXU_N],
                                           preferred_element_type=F32).astype(o_ref.dtype)


def _mm(x, w, out_dtype=F32, tile=TOK_TILE):
    n, k = x.shape
    m = w.shape[1]
    assert n % tile == 0 and m % (2 * MXU_N) == 0
    return pl.pallas_call(
        _mm_kernel, grid=(n // tile,),
        in_specs=[pl.BlockSpec((tile, k), lambda i: (i, 0)), pl.BlockSpec((k, m), lambda i: (0, 0))],
        out_specs=pl.BlockSpec((tile, m), lambda i: (i, 0)),
        out_shape=jax.ShapeDtypeStruct((n, m), out_dtype),
        compiler_params=_cparams(1), name="mm")(x, w)


def _layer_norm_rows(z, g, b):
    mu = jnp.mean(z, -1, keepdims=True)
    zc = z - mu
    var = jnp.mean(zc * zc, -1, keepdims=True)
    return zc * lax.rsqrt(var + LN_EPS) * g + b


def _proj_ln_kernel(a_ref, w_ref, x_ref, g_ref, b_ref, o_ref):
    y = jnp.dot(a_ref[...].astype(BF16), w_ref[...], preferred_element_type=F32)
    o_ref[...] = _layer_norm_rows(ALPHA * x_ref[...] + y, g_ref[...], b_ref[...])


def _proj_ln(a, w, x, g, b, tile=TOK_TILE):
    n, k = a.shape
    d = w.shape[1]
    return pl.pallas_call(
        _proj_ln_kernel, grid=(n // tile,),
        in_specs=[pl.BlockSpec((tile, k), lambda i: (i, 0)), pl.BlockSpec((k, d), lambda i: (0, 0)),
                  pl.BlockSpec((tile, d), lambda i: (i, 0)), pl.BlockSpec((1, d), lambda i: (0, 0)),
                  pl.BlockSpec((1, d), lambda i: (0, 0))],
        out_specs=pl.BlockSpec((tile, d), lambda i: (i, 0)),
        out_shape=jax.ShapeDtypeStruct((n, d), F32),
        compiler_params=_cparams(1), name="proj_ln")(a, w, x, g.reshape(1, d), b.reshape(1, d))


def _router_kernel(x_ref, wrt_ref, bias_ref, tri_ref, cls_ref, rank_ref, cnt_ref, carry_ref):
    @pl.when(pl.program_id(0) == 0)
    def _():
        carry_ref[...] = jnp.zeros_like(carry_ref)

    tile = x_ref.shape[0]
    logits = lax.dot_general(wrt_ref[...], x_ref[...].astype(BF16), (((1,), (1,)), ((), ())),
                             preferred_element_type=F32)
    ssel = jax.nn.sigmoid(logits) + bias_ref[...]
    rows = [ssel[e:e + 1, :] for e in range(N_EXPERTS)]

    def top2sum(a, b, c, d):
        return jnp.maximum(jnp.maximum(jnp.maximum(a + b, a + c), jnp.maximum(a + d, b + c)),
                           jnp.maximum(b + d, c + d))

    gscore = [top2sum(*rows[EPG * g:EPG * (g + 1)]) for g in range(N_GROUPS)]
    best, gi = gscore[0], jnp.zeros((1, tile), I32)
    for g in range(1, N_GROUPS):
        better = gscore[g] > best
        gi = jnp.where(better, g, gi)
        best = jnp.where(better, gscore[g], best)
    v = []
    for j in range(EPG):
        vj = rows[j]
        for g in range(1, N_GROUPS):
            vj = jnp.where(gi == g, rows[EPG * g + j], vj)
        v.append(vj)
    sel = []
    for i in range(EPG):
        r = jnp.zeros((1, tile), I32)
        for j in range(EPG):
            if j == i:
                continue
            beats = (v[j] > v[i]) | ((v[j] == v[i]) if j < i else False)
            r = r + beats.astype(I32)
        sel.append(r < 2)
    lo = jnp.where(sel[0], 0, jnp.where(sel[1], 1, 2))
    hi = jnp.where(sel[3], 3, jnp.where(sel[2], 2, 1))
    base = jnp.where(lo == 0, 0, jnp.where(lo == 1, 3, 5))
    cls = gi * N_PAIRS + base + hi - lo - 1

    onehot = (lax.broadcasted_iota(I32, (CLS_PAD, tile), 0) == cls).astype(F32)
    prefix = jnp.dot(onehot.astype(BF16), tri_ref[...], preferred_element_type=F32)
    carry = carry_ref[...]
    rank = jnp.sum(onehot * (prefix - 1.0 + carry), axis=0, keepdims=True)
    carry = carry + jnp.sum(onehot, axis=1, keepdims=True)
    carry_ref[...] = carry
    cls_ref[...] = cls
    rank_ref[...] = rank.astype(I32)
    cnt_ref[...] = carry.astype(I32)


def _route(x, wrt, bias, tile=TOK_TILE):
    n, d = x.shape
    tri = (np.arange(tile)[:, None] <= np.arange(tile)[None, :]).astype(np.float32)
    return pl.pallas_call(
        _router_kernel, grid=(n // tile,),
        in_specs=[pl.BlockSpec((tile, d), lambda i: (i, 0)), pl.BlockSpec((N_EXPERTS, d), lambda i: (0, 0)),
                  pl.BlockSpec((N_EXPERTS, 1), lambda i: (0, 0)), pl.BlockSpec((tile, tile), lambda i: (0, 0))],
        out_specs=[pl.BlockSpec((1, tile), lambda i: (0, i)), pl.BlockSpec((1, tile), lambda i: (0, i)),
                   pl.BlockSpec((CLS_PAD, 1), lambda i: (0, 0))],
        out_shape=[jax.ShapeDtypeStruct((1, n), I32), jax.ShapeDtypeStruct((1, n), I32),
                   jax.ShapeDtypeStruct((CLS_PAD, 1), I32)],
        scratch_shapes=[pltpu.VMEM((CLS_PAD, 1), F32)],
        compiler_params=_cparams(1), name="moe_route")(x, wrt, bias, jnp.asarray(tri, BF16))


def _row_copy(src_ref, src_row, dst_ref, dst_row, sem):
    return pltpu.make_async_copy(src_ref.at[pl.ds(src_row, 1)], dst_ref.at[pl.ds(dst_row, 1)], sem)


def _scatter_kernel(off_ref, cls_ref, rank_ref, x_ref, xs_in_ref, xs_ref, sem):
    del xs_in_ref
    tile = x_ref.shape[0]

    def start(r, c):
        dst = off_ref[cls_ref[0, r]] + rank_ref[0, r]
        _row_copy(x_ref, r, xs_ref, dst, sem).start()
        return c

    lax.fori_loop(0, tile, start, 0)

    def wait(r, c):
        _row_copy(x_ref, 0, xs_ref, 0, sem).wait()
        return c

    lax.fori_loop(0, tile, wait, 0)


def _scatter_rows(x, cls, rank, off, n_slots, tile=MOE_TILE):
    n, d = x.shape
    smem_row = pl.BlockSpec((1, tile), lambda i, off: (0, i), memory_space=pltpu.SMEM)
    return pl.pallas_call(
        _scatter_kernel,
        grid_spec=pltpu.PrefetchScalarGridSpec(
            num_scalar_prefetch=1, grid=(n // tile,),
            in_specs=[smem_row, smem_row, pl.BlockSpec((tile, d), lambda i, off: (i, 0)),
                      pl.BlockSpec(memory_space=pl.ANY)],
            out_specs=pl.BlockSpec(memory_space=pl.ANY),
            scratch_shapes=[pltpu.SemaphoreType.DMA(())]),
        out_shape=jax.ShapeDtypeStruct((n_slots, d), F32),
        input_output_aliases={4: 0},
        compiler_params=_cparams(1), name="moe_scatter")(off, cls, rank, x, jnp.zeros((n_slots, d), F32))


def _expert_kernel(e1_ref, e2_ref, valid_ref, xs_ref, wr_ref, g1_ref, u1_ref, d1_ref, g2_ref, u2_ref, d2_ref,
                   o_ref):
    j = pl.program_id(0)

    @pl.when(valid_ref[j] == 0)
    def _():
        o_ref[...] = jnp.zeros_like(o_ref)

    @pl.when(valid_ref[j] != 0)
    def _():
        xb = xs_ref[...].astype(BF16)
        s = jax.nn.sigmoid(jnp.dot(xb, wr_ref[...], preferred_element_type=F32))
        lane = lax.broadcasted_iota(I32, s.shape, 1)
        w1 = jnp.sum(jnp.where(lane == e1_ref[j], s, 0.0), axis=1, keepdims=True)
        w2 = jnp.sum(jnp.where(lane == e2_ref[j], s, 0.0), axis=1, keepdims=True)
        tot = w1 + w2

        def mlp(g_ref, u_ref, d_ref):
            hg = jnp.dot(xb, g_ref[0], preferred_element_type=F32)
            hu = jnp.dot(xb, u_ref[0], preferred_element_type=F32)
            hdn = hg * jax.nn.sigmoid(hg) * hu
            return jnp.dot(hdn.astype(BF16), d_ref[0], preferred_element_type=F32)

        o_ref[...] = (w1 / tot) * mlp(g1_ref, u1_ref, d1_ref) + (w2 / tot) * mlp(g2_ref, u2_ref, d2_ref)


def _expert_pairs(xs, wr, wg, wu, wd, e1, e2, valid, tile=MOE_TILE):
    n_slots, d = xs.shape
    de = wg.shape[2]
    up = lambda sel: pl.BlockSpec((1, d, de), lambda j, e1, e2, v: ((e1, e2)[sel][j], 0, 0))
    down = lambda sel: pl.BlockSpec((1, de, d), lambda j, e1, e2, v: ((e1, e2)[sel][j], 0, 0))
    return pl.pallas_call(
        _expert_kernel,
        grid_spec=pltpu.PrefetchScalarGridSpec(
            num_scalar_prefetch=3, grid=(n_slots // tile,),
            in_specs=[pl.BlockSpec((tile, d), lambda j, e1, e2, v: (j, 0)),
                      pl.BlockSpec((d, N_EXPERTS), lambda j, e1, e2, v: (0, 0)),
                      up(0), up(0), down(0), up(1), up(1), down(1)],
            out_specs=pl.BlockSpec((tile, d), lambda j, e1, e2, v: (j, 0))),
        out_shape=jax.ShapeDtypeStruct((n_slots, d), F32),
        compiler_params=_cparams(1), name="moe_experts")(e1, e2, valid, xs, wr, wg, wu, wd, wg, wu, wd)


def _gather_ln_kernel(off_ref, cls_ref, rank_ref, x_ref, ys_ref, g_ref, b_ref, o_ref, ybuf, sem):
    tile = x_ref.shape[0]

    def start(r, c):
        src = off_ref[cls_ref[0, r]] + rank_ref[0, r]
        _row_copy(ys_ref, src, ybuf, r, sem).start()
        return c

    lax.fori_loop(0, tile, start, 0)

    def wait(r, c):
        _row_copy(ys_ref, 0, ybuf, 0, sem).wait()
        return c

    lax.fori_loop(0, tile, wait, 0)
    o_ref[...] = _layer_norm_rows(ALPHA * x_ref[...] + ybuf[...], g_ref[...], b_ref[...])


def _gather_ln(x, ys, cls, rank, off, g, b, tile=MOE_TILE):
    n, d = x.shape
    smem_row = pl.BlockSpec((1, tile), lambda i, off: (0, i), memory_space=pltpu.SMEM)
    vec = pl.BlockSpec((1, d), lambda i, off: (0, 0))
    return pl.pallas_call(
        _gather_ln_kernel,
        grid_spec=pltpu.PrefetchScalarGridSpec(
            num_scalar_prefetch=1, grid=(n // tile,),
            in_specs=[smem_row, smem_row, pl.BlockSpec((tile, d), lambda i, off: (i, 0)),
                      pl.BlockSpec(memory_space=pl.ANY), vec, vec],
            out_specs=pl.BlockSpec((tile, d), lambda i, off: (i, 0)),
            scratch_shapes=[pltpu.VMEM((tile, d), F32), pltpu.SemaphoreType.DMA(())]),
        out_shape=jax.ShapeDtypeStruct((n, d), F32),
        compiler_params=_cparams(1), name="moe_gather_ln")(off, cls, rank, x, ys, g.reshape(1, d), b.reshape(1, d))


_PAIR_LO = np.array([0, 0, 0, 1, 1, 2], np.int32)
_PAIR_HI = np.array([1, 2, 3, 2, 3, 3], np.int32)


def _moe_ln(x, w_router, router_bias, wg, wu, wd, g, b):
    n, d = x.shape
    cls, rank, cnt = _route(x, w_router.T.astype(BF16), router_bias.reshape(N_EXPERTS, 1).astype(F32))
    cnt = cnt[:N_CLASSES, 0]
    tiles = (cnt + MOE_TILE - 1) // MOE_TILE
    tile_end = jnp.cumsum(tiles)
    off = jnp.zeros((CLS_PAD,), I32).at[:N_CLASSES].set((tile_end - tiles) * MOE_TILE)
    n_tiles = n // MOE_TILE + N_CLASSES
    tile_id = jnp.arange(n_tiles, dtype=I32)
    tile_cls = jnp.minimum(jnp.sum(tile_id[:, None] >= tile_end[None, :], axis=1), N_CLASSES - 1).astype(I32)
    valid = (tile_id < tile_end[-1]).astype(I32)
    grp, pair = tile_cls // N_PAIRS, tile_cls % N_PAIRS
    e1 = grp * EPG + jnp.asarray(_PAIR_LO)[pair]
    e2 = grp * EPG + jnp.asarray(_PAIR_HI)[pair]
    xs = _scatter_rows(x, cls, rank, off, n_tiles * MOE_TILE)
    ys = _expert_pairs(xs, w_router.astype(BF16), wg, wu, wd, e1, e2, valid)
    return _gather_ln(x, ys, cls, rank, off, g, b)


def _prep_w_in_ab(w):
    k = w.shape[0]
    q, kv, gate, qr, kr, vr, gr = jnp.split(w, np.cumsum([512, 768, 24, 512, 512, 512]).tolist(), axis=1)
    pad = jnp.zeros((k, AB_W - AB_GATE - 24), w.dtype)
    return jnp.concatenate([q, kv, qr, kr, vr, gr, gate, pad], axis=1).astype(BF16)


def _prep_compress(w1, w2, pe):
    hd = NSA_HD
    w1r = w1.reshape(2, 2, CMP_STRIDE, hd, CMP_HIDDEN)
    w1bd = jnp.zeros((2, 2, CMP_STRIDE, NSA_G * hd, NSA_G * CMP_HIDDEN), F32)
    w2bd = jnp.zeros((2, NSA_G * CMP_HIDDEN, NSA_G * hd), F32)
    for g in range(NSA_G):
        w1bd = w1bd.at[..., g * hd:(g + 1) * hd, g * CMP_HIDDEN:(g + 1) * CMP_HIDDEN].set(w1r)
        w2bd = w2bd.at[:, g * CMP_HIDDEN:(g + 1) * CMP_HIDDEN, g * hd:(g + 1) * hd].set(w2)
    per = pe.reshape(2, 2, CMP_STRIDE, hd)
    pe_l = jnp.concatenate([per] * NSA_G, axis=-1)
    return w1bd.astype(BF16), w2bd.astype(BF16), pe_l


def _compress_rows(x_ref, kind, n_chunks, pe_ref, w1_ref, w2_ref):
    acc = [jnp.zeros((n_chunks, NSA_G * CMP_HIDDEN), F32) for _ in range(2)]
    for s in range(CMP_STRIDE):
        xs = x_ref[pl.ds(s, n_chunks, stride=CMP_STRIDE), :]
        for j in range(2):
            acc[j] = acc[j] + jnp.dot((xs + pe_ref[kind, j, s:s + 1, :]).astype(BF16), w1_ref[kind, j, s],
                                      preferred_element_type=F32)
    hdn = acc[0] + pltpu.roll(acc[1], n_chunks - 1, 0)
    return jnp.dot(jax.nn.gelu(hdn).astype(BF16), w2_ref[kind], preferred_element_type=F32)


def _compress_kernel(k_ref, v_ref, pe_ref, w1_ref, w2_ref, o_ref):
    n_chunks = o_ref.shape[1]
    o_ref[0, :, 0:LANES] = _compress_rows(k_ref, 0, n_chunks, pe_ref, w1_ref, w2_ref)
    o_ref[0, :, LANES:2 * LANES] = _compress_rows(v_ref, 1, n_chunks, pe_ref, w1_ref, w2_ref)


def _compress_prompt(h, batch, seq, w1bd, w2bd, pe_l):
    n_chunks = seq // CMP_STRIDE
    full = lambda a: pl.BlockSpec(a.shape, lambda b: (0,) * a.ndim)
    return pl.pallas_call(
        _compress_kernel, grid=(batch,),
        in_specs=[pl.BlockSpec((seq, LANES), lambda b: (b, AB_KV // LANES)),
                  pl.BlockSpec((seq, LANES), lambda b: (b, AB_KV // LANES + 1)), full(pe_l), full(w1bd), full(w2bd)],
        out_specs=pl.BlockSpec((1, n_chunks, 256), lambda b: (b, 0, 0)),
        out_shape=jax.ShapeDtypeStruct((batch, n_chunks, 256), F32),
        compiler_params=_cparams(1), name="nsa_compress")(h, h, pe_l, w1bd, w2bd)


def _masked_softmax(s, mask):
    s = jnp.where(mask, s, -jnp.inf)
    m = jnp.max(s, -1, keepdims=True)
    m = jnp.where(m == -jnp.inf, 0.0, m)
    e = jnp.exp(s - m)
    den = jnp.maximum(jnp.sum(e, -1, keepdims=True), jnp.finfo(F32).tiny)
    return e * (1.0 / den)


def _dot_nt(a, b):
    return lax.dot_general(a, b, (((1,), (1,)), ((), ())), preferred_element_type=F32)


def _topn_mask(imp, n_blocks, n_top):
    lane = lax.broadcasted_iota(I32, imp.shape, 1)
    rank = jnp.zeros(imp.shape, I32)
    for j in range(n_blocks):
        col = imp[:, j:j + 1]
        rank = rank + ((col > imp) | ((col == imp) & (lane > j))).astype(I32)
    return (rank < n_top) & (lane < n_blocks)


def _nsa_prompt_kernel(q_ref, gate_ref, slc_ref, win_ref, kvc_ref, selmap_ref, expand_ref, o_ref, *, seq, win_span):
    qb = q_ref.shape[0]
    i0 = pl.program_id(1) * qb
    n_cmp_pad = kvc_ref.shape[1]
    n_slc = seq // SEL_BLOCK
    scale = NSA_HD ** -0.5
    qpos = i0 + lax.broadcasted_iota(I32, (qb, 1), 0)
    lane = lax.broadcasted_iota(I32, (qb, LANES), 1)
    gates = jax.nn.sigmoid(gate_ref[:, 0:LANES])

    kvc = kvc_ref[0].astype(BF16)
    cmp_last = lax.broadcasted_iota(I32, (1, n_cmp_pad), 1) * CMP_STRIDE + (CMP_BLOCK - 1)
    cmp_mask = cmp_last <= qpos
    kpos = lax.broadcasted_iota(I32, (1, seq), 1)
    causal = kpos <= qpos
    w0 = jnp.maximum(i0 + qb - win_span, 0)
    w0 = pl.multiple_of(w0, qb)
    wpos = w0 + lax.broadcasted_iota(I32, (1, win_span), 1)
    wmask = (wpos <= qpos) & (wpos > qpos - NSA_WINDOW)
    k_slc = slc_ref[:, 0:LANES].astype(BF16)
    v_slc = slc_ref[:, LANES:2 * LANES].astype(BF16)
    k_win = win_ref[pl.ds(w0, win_span), 0:LANES].astype(BF16)
    v_win = win_ref[pl.ds(w0, win_span), LANES:2 * LANES].astype(BF16)

    out_tiles = [None] * (NSA_H // 2)
    for g in range(NSA_G):
        heads = range(g * NSA_HPG, (g + 1) * NSA_HPG)
        q128, p_cmp = {}, {}
        imp = jnp.zeros((qb, LANES), F32)
        for h in heads:
            t = q_ref[:, (h // 2) * LANES:(h // 2 + 1) * LANES] * scale
            t = jnp.where((lane // NSA_HD) == (h % 2), t, 0.0)
            if h % 2 != g:
                t = pltpu.roll(t, NSA_HD, 1)
            q128[h] = t.astype(BF16)
            p = _masked_softmax(_dot_nt(q128[h], kvc[:, 0:LANES]), cmp_mask)
            p_cmp[h] = p.astype(BF16)
            imp = imp + jnp.dot(p_cmp[h], selmap_ref[...], preferred_element_type=F32)
        cur = qpos // SEL_BLOCK
        forced = (lane == 0) | ((lane <= cur) & (lane > cur - SEL_LOCAL))
        imp = jnp.where(forced, FORCE_SCORE, imp)
        imp = jnp.where(lane * SEL_BLOCK <= qpos, imp, -jnp.inf)
        sel = _topn_mask(imp, n_slc, min(SEL_TOPN, n_slc))
        keymask = (jnp.dot(sel.astype(BF16), expand_ref[...], preferred_element_type=F32) > 0.5) & causal
        for h in heads:
            o_cmp = jnp.dot(p_cmp[h], kvc[:, LANES:2 * LANES], preferred_element_type=F32)
            p = _masked_softmax(_dot_nt(q128[h], k_slc), keymask)
            o_slc = jnp.dot(p.astype(BF16), v_slc, preferred_element_type=F32)
            p = _masked_softmax(_dot_nt(q128[h], k_win), wmask)
            o_win = jnp.dot(p.astype(BF16), v_win, preferred_element_type=F32)
            o = (gates[:, 3 * h:3 * h + 1] * o_cmp + gates[:, 3 * h + 1:3 * h + 2] * o_slc
                 + gates[:, 3 * h + 2:3 * h + 3] * o_win)
            if h % 2 != g:
                o = pltpu.roll(o, NSA_HD, 1)
            keep = (lane // NSA_HD) == (h % 2)
            prev = out_tiles[h // 2]
            out_tiles[h // 2] = jnp.where(keep, o, 0.0 if prev is None else prev)
    for t, tile in enumerate(out_tiles):
        o_ref[:, t * LANES:(t + 1) * LANES] = tile


def _nsa_prompt(h, kvc, batch, seq):
    nqb = seq // Q_BLOCK
    n_cmp_pad = seq // CMP_STRIDE
    n_slc = seq // SEL_BLOCK
    assert seq % Q_BLOCK == 0 and n_slc <= LANES
    win_span = min(NSA_WINDOW + Q_BLOCK, seq)
    ci = np.arange(n_cmp_pad)[:, None] * CMP_STRIDE
    sj = np.arange(LANES)[None, :] * SEL_BLOCK
    selmap = np.clip(np.minimum(ci + CMP_BLOCK, sj + SEL_BLOCK) - np.maximum(ci, sj), 0, None) / CMP_STRIDE
    selmap[n_cmp_pad - 1:, :] = 0.0
    selmap[:, n_slc:] = 0.0
    expand = (np.arange(seq)[None, :] // SEL_BLOCK == np.arange(LANES)[:, None]).astype(np.float32)
    const = lambda a: pl.BlockSpec(a.shape, lambda b, i: (0,) * a.ndim)
    selmap, expand = jnp.asarray(selmap, BF16), jnp.asarray(expand, BF16)
    return pl.pallas_call(
        functools.partial(_nsa_prompt_kernel, seq=seq, win_span=win_span), grid=(batch, nqb),
        in_specs=[pl.BlockSpec((Q_BLOCK, 512), lambda b, i: (b * nqb + i, AB_Q // 512)),
                  pl.BlockSpec((Q_BLOCK, 256), lambda b, i: (b * nqb + i, AB_GATE // 256)),
                  pl.BlockSpec((seq, 256), lambda b, i: (b, (AB_KV + 256) // 256)),
                  pl.BlockSpec((seq, 256), lambda b, i: (b, (AB_KV + 512) // 256)),
                  pl.BlockSpec((1, n_cmp_pad, 256), lambda b, i: (b, 0, 0)),
                  const(selmap), const(expand)],
        out_specs=pl.BlockSpec((Q_BLOCK, 512), lambda b, i: (b * nqb + i, 0)),
        out_shape=jax.ShapeDtypeStruct((batch * seq, 512), F32),
        compiler_params=_cparams(2), name="nsa_prompt")(h, h, h, h, kvc, selmap, expand)


def _rope_tables(pos, half):
    inv = ROPE_THETA ** (-jnp.arange(half, dtype=F32) / half)
    ang = pos.astype(F32)[:, None] * inv[None, :]
    cos, sin = jnp.cos(ang), jnp.sin(ang)
    return jnp.concatenate([cos, cos], -1), jnp.concatenate([-sin, sin], -1)


def _retention_tables(chunk):
    log_g = jnp.log1p(-jnp.exp2(-5.0 - jnp.arange(RET_H, dtype=F32)))
    t = jnp.arange(chunk, dtype=F32)
    diff = t[:, None] - t[None, :]
    decay = jnp.where(diff >= 0, jnp.exp(log_g[:, None, None] * jnp.maximum(diff, 0.0)), 0.0)
    xi = jnp.exp(log_g[:, None] * (t + 1.0))
    zeta = jnp.exp(log_g[:, None] * (chunk - 1.0 - t))
    g_chunk = jnp.broadcast_to(jnp.exp(log_g * chunk)[:, None], (RET_H, chunk))
    cols = jnp.stack([xi, zeta, g_chunk], axis=-1)
    return decay, jnp.pad(cols, ((0, 0), (0, 0), (0, LANES - 3)))


def _head_norm_gate(o, gate):
    mu = jnp.mean(o, -1, keepdims=True)
    oc = o - mu
    var = jnp.mean(oc * oc, -1, keepdims=True)
    return oc * lax.rsqrt(var + LN_EPS) * (gate * jax.nn.sigmoid(gate))


def _rope_rows(x, cos, sin):
    return x * cos + pltpu.roll(x, x.shape[-1] // 2, 1) * sin


def _ret_prompt_kernel(q_ref, k_ref, v_ref, g_ref, cos_ref, sin_ref, decay_ref, cols_ref, o_ref, s_ref, *, chunk):
    n_chunks = q_ref.shape[0] // chunk
    decay = decay_ref[0]
    xi, zeta, g_chunk = cols_ref[0, :, 0:1], cols_ref[0, :, 1:2], cols_ref[0, 0:1, 2:3]

    def body(c, state):
        r = pl.ds(pl.multiple_of(c * chunk, chunk), chunk)
        cos, sin = cos_ref[r, :], sin_ref[r, :]
        q = _rope_rows(q_ref[r, :], cos, sin)
        k = _rope_rows(k_ref[r, :], cos, sin) * (RET_DK ** -0.5)
        v = v_ref[r, :].astype(BF16)
        inner = _dot_nt(q.astype(BF16), k.astype(BF16)) * decay
        o = (jnp.dot(inner.astype(BF16), v, preferred_element_type=F32)
             + jnp.dot((q * xi).astype(BF16), state.astype(BF16), preferred_element_type=F32))
        o_ref[r, :] = _head_norm_gate(o, g_ref[r, :])
        kz = (k * zeta).astype(BF16)
        return g_chunk * state + lax.dot_general(kz, v, (((0,), (0,)), ((), ())), preferred_element_type=F32)

    s_ref[0, 0] = lax.fori_loop(0, n_chunks, body, jnp.zeros((RET_DK, RET_DV), F32))


def _ret_prompt(h, batch, seq):
    chunk = math.gcd(seq, RET_CHUNK)
    assert chunk == RET_CHUNK and RET_DK == LANES and RET_DV == LANES
    cos, sin = _rope_tables(jnp.arange(seq), RET_DK // 2)
    decay, cols = _retention_tables(chunk)
    col = lambda base: pl.BlockSpec((seq, LANES), lambda b, hh: (b, base // LANES + hh))
    tab = pl.BlockSpec((seq, LANES), lambda b, hh: (0, 0))
    return pl.pallas_call(
        functools.partial(_ret_prompt_kernel, chunk=chunk), grid=(batch, RET_H),
        in_specs=[col(AB_QR), col(AB_KR), col(AB_VR), col(AB_GR), tab, tab,
                  pl.BlockSpec((1, chunk, chunk), lambda b, hh: (hh, 0, 0)),
                  pl.BlockSpec((1, chunk, LANES), lambda b, hh: (hh, 0, 0))],
        out_specs=[pl.BlockSpec((seq, LANES), lambda b, hh: (b, hh)),
                   pl.BlockSpec((1, 1, RET_DK, RET_DV), lambda b, hh: (b, hh, 0, 0))],
        out_shape=[jax.ShapeDtypeStruct((batch * seq, RET_H * RET_DV), F32),
                   jax.ShapeDtypeStruct((batch, RET_H, RET_DK, RET_DV), F32)],
        compiler_params=_cparams(2), name="ret_prompt")(h, h, h, h, cos, sin, decay, cols)


def _dev_mixer_ab_prompt(x, w_in, w_out, w_cmp1, w_cmp2, cmp_pe):
    batch, seq, d = x.shape
    h = _mm(x.reshape(batch * seq, d), _prep_w_in_ab(w_in))
    w1bd, w2bd, pe_l = _prep_compress(w_cmp1, w_cmp2, cmp_pe)
    kvc = _compress_prompt(h, batch, seq, w1bd, w2bd, pe_l)
    o_nsa = _nsa_prompt(h, kvc, batch, seq)
    o_ret, s_new = _ret_prompt(h, batch, seq)
    mixed = jnp.concatenate([o_nsa, o_ret], axis=1)
    mix = _mm(mixed, w_out.astype(BF16))
    new_kv = h[:, AB_KV:AB_KV + 512].reshape(batch, seq, 4, NSA_G, NSA_HD)
    n_keep = min(NSA_WINDOW, seq)
    win = h[:, AB_KV + 512:AB_KV + 768].reshape(batch, seq, 2, NSA_G, NSA_HD)[:, seq - n_keep:]
    return mix.reshape(batch, seq, d), new_kv, win, s_new


def _page_copy(cache_ref, page, col0, width, dst_ref, sem):
    return pltpu.make_async_copy(cache_ref.at[page, :, pl.ds(col0, width)], dst_ref, sem)


def _topn_ids(imp, n_blocks, n_top, out_lane, lane0):
    lane = lax.broadcasted_iota(I32, imp.shape, 1)
    taken = lane >= n_blocks
    ids = jnp.zeros(out_lane.shape, I32)
    for t in range(n_top):
        m = jnp.max(jnp.where(taken, -jnp.inf, imp), axis=1, keepdims=True)
        idx = jnp.min(jnp.where(jnp.logical_not(taken) & (imp >= m), lane.astype(F32), float(imp.shape[1])),
                      axis=1, keepdims=True).astype(I32)
        ids = jnp.where(out_lane == lane0 + t, idx, ids)
        taken = taken | (lane == idx)
    return ids


def _nsa_sample_cmp_kernel(tbl_ref, cache_ref, q_ref, pe_ref, w1_ref, w2_ref, selmap_ref, ocmp_ref, ids_ref,
                           buf, sem, *, n_pages, past):
    b = pl.program_id(0)
    nb = pl.num_programs(0)
    slot = b % 2

    def page(bb, p, kind, sl):
        return _page_copy(cache_ref, tbl_ref[bb * n_pages + p], kind * LANES, LANES,
                          buf.at[sl, kind, pl.ds(p * PAGE, PAGE)], sem.at[sl])

    def fetch(bb, sl):
        def body(p, c):
            page(bb, p, 0, sl).start()
            page(bb, p, 1, sl).start()
            return c
        lax.fori_loop(0, n_pages, body, 0)

    @pl.when(b == 0)
    def _():
        fetch(b, slot)

    @pl.when(b + 1 < nb)
    def _():
        fetch(b + 1, 1 - slot)

    def wait(p, c):
        page(b, 0, 0, slot).wait()
        page(b, 0, 1, slot).wait()
        return c
    lax.fori_loop(0, n_pages, wait, 0)

    n_chunks = past // CMP_STRIDE
    n_cmp = n_chunks - CMP_BLOCK // CMP_STRIDE + 1
    n_slc = past // SEL_BLOCK + 1
    k_cmp = _compress_rows(buf.at[slot, 0], 0, n_chunks, pe_ref, w1_ref, w2_ref).astype(BF16)
    v_cmp = _compress_rows(buf.at[slot, 1], 1, n_chunks, pe_ref, w1_ref, w2_ref).astype(BF16)
    q = (q_ref[0] * (NSA_HD ** -0.5)).astype(BF16)
    cmp_idx = lax.broadcasted_iota(I32, (1, n_chunks), 1)
    cmp_mask = (cmp_idx < n_cmp) & (cmp_idx * CMP_STRIDE + (CMP_BLOCK - 1) <= past)
    p = _masked_softmax(_dot_nt(q, k_cmp), cmp_mask).astype(BF16)
    ocmp_ref[0] = jnp.dot(p, v_cmp, preferred_element_type=F32)
    imp_h = jnp.dot(p, selmap_ref[...], preferred_element_type=F32)
    lane = lax.broadcasted_iota(I32, (1, imp_h.shape[1]), 1)
    cur = past // SEL_BLOCK
    forced = (lane == 0) | ((lane <= cur) & (lane > cur - SEL_LOCAL))
    out_lane = lax.broadcasted_iota(I32, (1, LANES), 1)
    ids = jnp.zeros((1, LANES), I32)
    for g in range(NSA_G):
        imp = jnp.sum(imp_h[g * NSA_HPG:(g + 1) * NSA_HPG], axis=0, keepdims=True)
        imp = jnp.where(forced, FORCE_SCORE, imp)
        imp = jnp.where(lane * SEL_BLOCK <= past, imp, -jnp.inf)
        ids = ids + _topn_ids(imp, n_slc, min(SEL_TOPN, n_slc), out_lane, g * SEL_TOPN)
    ids_ref[0] = ids


def _nsa_sample_cmp(table, cache, q128, w1bd, w2bd, pe_l, past):
    batch, n_pages = table.shape
    n_chunks = past // CMP_STRIDE
    n_slc = past // SEL_BLOCK + 1
    n_slc_pad = -(-n_slc // LANES) * LANES
    ci = np.arange(n_chunks)[:, None] * CMP_STRIDE
    sj = np.arange(n_slc_pad)[None, :] * SEL_BLOCK
    selmap = np.clip(np.minimum(ci + CMP_BLOCK, sj + SEL_BLOCK) - np.maximum(ci, sj), 0, None) / CMP_STRIDE
    selmap[n_chunks - 1:, :] = 0.0
    selmap[:, n_slc:] = 0.0
    selmap = jnp.asarray(selmap, BF16)
    const = lambda a: pl.BlockSpec(a.shape, lambda b, t: (0,) * a.ndim)
    return pl.pallas_call(
        functools.partial(_nsa_sample_cmp_kernel, n_pages=n_pages, past=past),
        grid_spec=pltpu.PrefetchScalarGridSpec(
            num_scalar_prefetch=1, grid=(batch,),
            in_specs=[pl.BlockSpec(memory_space=pl.ANY), pl.BlockSpec((1, NSA_H, LANES), lambda b, t: (b, 0, 0)),
                      const(pe_l), const(w1bd), const(w2bd), const(selmap)],
            out_specs=[pl.BlockSpec((1, NSA_H, LANES), lambda b, t: (b, 0, 0)),
                       pl.BlockSpec((1, 1, LANES), lambda b, t: (b, 0, 0))],
            scratch_shapes=[pltpu.VMEM((2, 2, past, LANES), F32), pltpu.SemaphoreType.DMA((2,))]),
        out_shape=[jax.ShapeDtypeStruct((batch, NSA_H, LANES), F32), jax.ShapeDtypeStruct((batch, 1, LANES), I32)],
        compiler_params=_cparams(1), name="nsa_sample_cmp")(table.reshape(-1), cache, q128, pe_l, w1bd, w2bd, selmap)


def _nsa_sample_attend_kernel(tbl_ref, ids_ref, cache_ref, q_ref, ocmp_ref, gate_ref, new_ref, win_ref,
                              o_ref, wout_ref, sbuf, sem, *, n_pages, past):
    b = pl.program_id(0)
    nb = pl.num_programs(0)
    slot = b % 2
    n_sel = NSA_G * SEL_TOPN
    last_blk = past // SEL_BLOCK
    half = SEL_BLOCK
    per_page = PAGE // SEL_BLOCK

    def block_copy(bb, i, sl):
        blk = jnp.minimum(ids_ref[bb * n_sel + i], last_blk - 1)
        page = tbl_ref[bb * n_pages + blk // per_page]
        return pltpu.make_async_copy(
            cache_ref.at[page, pl.ds((blk % per_page) * half, half), pl.ds(256, 256)], sbuf.at[sl, i], sem.at[sl])

    def fetch(bb, sl):
        for i in range(n_sel):
            block_copy(bb, i, sl).start()

    @pl.when(b == 0)
    def _():
        fetch(b, slot)

    @pl.when(b + 1 < nb)
    def _():
        fetch(b + 1, 1 - slot)

    n_win = win_ref.shape[1]
    for i in range(n_sel):
        block_copy(b, i, slot).wait()

    q = (q_ref[0] * (NSA_HD ** -0.5))
    qb = q.astype(BF16)
    new = new_ref[0]
    gates = jax.nn.sigmoid(gate_ref[0])
    row = lax.broadcasted_iota(I32, (NSA_H, 1), 0)

    def attend(s, mask, v, s_new, v_new, new_ok):
        s = jnp.where(mask, s, -jnp.inf)
        s_new = jnp.where(new_ok, s_new, -jnp.inf)
        m = jnp.maximum(jnp.max(s, -1, keepdims=True), s_new)
        m = jnp.where(m == -jnp.inf, 0.0, m)
        e, e_new = jnp.exp(s - m), jnp.exp(s_new - m)
        den = jnp.maximum(jnp.sum(e, -1, keepdims=True) + e_new, jnp.finfo(F32).tiny)
        return (jnp.dot(e.astype(BF16), v, preferred_element_type=F32) + e_new * v_new) * (1.0 / den)

    o_slc = jnp.zeros((NSA_H, LANES), F32)
    key_lane = lax.broadcasted_iota(I32, (1, SEL_TOPN * half), 1)
    s_new = jnp.sum(q * new[:, 0:LANES], axis=-1, keepdims=True)
    for g in range(NSA_G):
        kv = sbuf[slot, g * SEL_TOPN:(g + 1) * SEL_TOPN].reshape(SEL_TOPN * half, 256)
        ok = jnp.zeros((1, SEL_TOPN * half), jnp.bool_)
        has_new = False
        for i in range(SEL_TOPN):
            blk = ids_ref[b * n_sel + g * SEL_TOPN + i]
            ok = ok | ((key_lane // half == i) & (blk != last_blk))
            has_new = has_new | (blk == last_blk)
        o = attend(_dot_nt(qb, kv[:, 0:LANES].astype(BF16)), ok, kv[:, LANES:2 * LANES].astype(BF16),
                   s_new, new[:, LANES:2 * LANES], has_new)
        o_slc = jnp.where(row // NSA_HPG == g, o, o_slc)

    win = win_ref[0]
    wpos = past - n_win + lax.broadcasted_iota(I32, (1, n_win), 1)
    wmask = (wpos >= 0) & (wpos > past - NSA_WINDOW)
    sw_new = jnp.sum(q * new[:, 2 * LANES:3 * LANES], axis=-1, keepdims=True)
    o_win = attend(_dot_nt(qb, win[:, 0:LANES].astype(BF16)), wmask, win[:, LANES:2 * LANES].astype(BF16),
                   sw_new, new[:, 3 * LANES:4 * LANES], True)

    o = gates[:, 0:1] * ocmp_ref[0] + gates[:, 1:2] * o_slc + gates[:, 2:3] * o_win
    o = jnp.where(row // NSA_HPG == 0, o, pltpu.roll(o, NSA_HD, 1))
    o_ref[0] = o[:, 0:NSA_HD]
    win_row = lax.broadcasted_iota(I32, (n_win, 1), 0)
    wout_ref[0] = jnp.where(win_row == n_win - 1, new[:, 2 * LANES:4 * LANES], pltpu.roll(win, n_win - 1, 0))


def _nsa_sample_attend(table, ids, cache, q128, o_cmp, gates, new, win, past):
    batch, n_pages = table.shape
    n_win = win.shape[1]
    assert n_win == NSA_WINDOW and past >= NSA_WINDOW
    per_b = lambda shape: pl.BlockSpec((1,) + shape, lambda b, t, i: (b, 0, 0))
    return pl.pallas_call(
        functools.partial(_nsa_sample_attend_kernel, n_pages=n_pages, past=past),
        grid_spec=pltpu.PrefetchScalarGridSpec(
            num_scalar_prefetch=2, grid=(batch,),
            in_specs=[pl.BlockSpec(memory_space=pl.ANY), per_b((NSA_H, LANES)), per_b((NSA_H, LANES)),
                      per_b((NSA_H, LANES)), per_b((1, 512)), per_b((n_win, 256))],
            out_specs=[per_b((NSA_H, NSA_HD)), per_b((n_win, 256))],
            scratch_shapes=[pltpu.VMEM((2, NSA_G * SEL_TOPN, SEL_BLOCK, 256), F32), pltpu.SemaphoreType.DMA((2,))]),
        out_shape=[jax.ShapeDtypeStruct((batch, NSA_H, NSA_HD), F32), jax.ShapeDtypeStruct(win.shape, F32)],
        compiler_params=_cparams(1), name="nsa_sample_attend")(
            table.reshape(-1), ids, cache, q128, o_cmp, gates, new, win)


def _ret_sample_kernel(q_ref, k_ref, v_ref, g_ref, cos_ref, sin_ref, gam_ref, s0_ref, o_ref, s_ref):
    cos, sin = cos_ref[...], sin_ref[...]
    q = _rope_rows(q_ref[0], cos, sin)
    k = _rope_rows(k_ref[0], cos, sin) * (RET_DK ** -0.5)
    v = v_ref[0]
    gam = gam_ref[...]
    inner = jnp.sum(q * k, axis=-1, keepdims=True)
    eye = lax.broadcasted_iota(I32, (RET_DK, RET_DK), 0) == lax.broadcasted_iota(I32, (RET_DK, RET_DK), 1)
    qx = (q * gam).astype(BF16)
    rows = []
    for h in range(RET_H):
        s0 = s0_ref[0, h]
        rows.append(jnp.dot(qx, s0.astype(BF16), preferred_element_type=F32)[h:h + 1])
        k_col = jnp.sum(jnp.where(eye, k[h:h + 1], 0.0), axis=1, keepdims=True)
        s_ref[0, h] = gam[h:h + 1, 0:1] * s0 + k_col * v[h:h + 1]
    o = inner * v + jnp.concatenate(rows, axis=0)
    o_ref[0] = _head_norm_gate(o, g_ref[0])


def _ret_sample(q, k, v, g, s0, past):
    batch = q.shape[0]
    cos, sin = _rope_tables(jnp.full((1,), past), RET_DK // 2)
    gam = jnp.exp(jnp.log1p(-jnp.exp2(-5.0 - jnp.arange(RET_H, dtype=F32))))[:, None] * jnp.ones((1, LANES), F32)
    row = pl.BlockSpec((1, RET_H, LANES), lambda b: (b, 0, 0))
    const = lambda a: pl.BlockSpec(a.shape, lambda b: (0,) * a.ndim)
    state = pl.BlockSpec((1, RET_H, RET_DK, RET_DV), lambda b: (b, 0, 0, 0))
    return pl.pallas_call(
        _ret_sample_kernel, grid=(batch,),
        in_specs=[row, row, row, row, const(cos), const(sin), const(gam), state],
        out_specs=[row, state],
        out_shape=[jax.ShapeDtypeStruct((batch, RET_H, RET_DV), F32), jax.ShapeDtypeStruct(s0.shape, F32)],
        compiler_params=_cparams(1), name="ret_sample")(q, k, v, g, cos, sin, gam, s0)


def _group_lanes(q):
    g = (jnp.arange(NSA_H) // NSA_HPG)[None, :, None]
    z = jnp.zeros_like(q)
    return jnp.concatenate([jnp.where(g == 0, q, z), jnp.where(g == 1, q, z)], axis=-1)


def _mixer_ab_sample(hs, cache, win, s0, table, w1bd, w2bd, pe_l):
    batch = hs.shape[0]
    past = table.shape[1] * PAGE
    cache = cache.reshape(cache.shape[0], PAGE, 4 * NSA_G * NSA_HD)
    q128 = _group_lanes(hs[:, AB_Q:AB_Q + 512].reshape(batch, NSA_H, NSA_HD))
    o_cmp, ids = _nsa_sample_cmp(table, cache, q128, w1bd, w2bd, pe_l, past)
    gates = jnp.pad(hs[:, AB_GATE:AB_GATE + 24].reshape(batch, NSA_H, 3), ((0, 0), (0, 0), (0, LANES - 3)))
    new = hs[:, AB_KV + 256:AB_KV + 768].reshape(batch, 1, 512)
    o_nsa, win_new = _nsa_sample_attend(table, ids[:, 0, :NSA_G * SEL_TOPN].reshape(-1), cache, q128, o_cmp, gates,
                                        new, win.reshape(batch, win.shape[1], 256), past)
    seg = lambda c: hs[:, c:c + 512].reshape(batch, RET_H, RET_DK)
    o_ret, s_new = _ret_sample(seg(AB_QR), seg(AB_KR), seg(AB_VR), seg(AB_GR), s0, past)
    mixed = jnp.concatenate([o_nsa.reshape(batch, 512), o_ret.reshape(batch, 512)], axis=1)
    return mixed, win_new.reshape(win.shape), s_new


def _dev_mixer_ab_sample(x, cache, win, s0, table, w_in, w_out, w_cmp1, w_cmp2, cmp_pe):
    batch = x.shape[0]
    xs = jnp.pad(x.reshape(batch, -1), ((0, TOK_TILE - batch), (0, 0)))
    hs = _mm(xs, _prep_w_in_ab(w_in))[:batch]
    w1bd, w2bd, pe_l = _prep_compress(w_cmp1, w_cmp2, cmp_pe)
    mixed, win_new, s_new = _mixer_ab_sample(hs, cache, win, s0, table, w1bd, w2bd, pe_l)
    mix = _mm(jnp.pad(mixed, ((0, TOK_TILE - batch), (0, 0))), w_out.astype(BF16))[:batch]
    new_kv = hs[:, AB_KV:AB_KV + 512].reshape(batch, 1, 4, NSA_G, NSA_HD)
    return mix.reshape(batch, 1, -1), new_kv, win_new, s_new


MLA_ROW_W = 384
MLA_QK = LANES
MLA_CW = 1024
MLA_KR, MLA_KR_ROT = 640, 768


def _rot_half_cols(w):
    half = w.shape[-1] // 2
    return jnp.concatenate([-w[..., half:], w[..., :half]], axis=-1)


def _prep_mla(w_in, w_qb, w_kb, w_vb):
    d = w_in.shape[0]
    kr = w_in[:, MLA_Q_LORA + MLA_KV_LORA:]
    z = lambda n: jnp.zeros((d, n), w_in.dtype)
    wc = jnp.concatenate([w_in[:, :MLA_Q_LORA + MLA_KV_LORA], kr, z(LANES - MLA_ROPE), _rot_half_cols(kr),
                          z(MLA_CW - MLA_KR_ROT - MLA_ROPE)], axis=1)
    zq = jnp.zeros((MLA_Q_LORA, MLA_H, MLA_QK - MLA_NOPE - MLA_ROPE), w_qb.dtype)
    wq = jnp.concatenate([w_qb, zq], axis=-1).reshape(MLA_Q_LORA, MLA_H * MLA_QK)
    wq_rot = jnp.concatenate([jnp.zeros_like(w_qb[..., :MLA_NOPE]), _rot_half_cols(w_qb[..., MLA_NOPE:]), zq],
                             axis=-1).reshape(MLA_Q_LORA, MLA_H * MLA_QK)
    eye = jnp.eye(MLA_ROPE, dtype=w_kb.dtype)
    wk = jnp.zeros((MLA_ROW_W, MLA_H, MLA_QK), w_kb.dtype)
    wk = wk.at[:MLA_KV_LORA, :, :MLA_NOPE].set(w_kb)
    wk = wk.at[MLA_KV_LORA:MLA_KV_LORA + MLA_ROPE, :, MLA_NOPE:MLA_NOPE + MLA_ROPE].set(
        jnp.broadcast_to(eye[:, None, :], (MLA_ROPE, MLA_H, MLA_ROPE)))
    wv = jnp.zeros((MLA_ROW_W, MLA_H, MLA_V), w_vb.dtype).at[:MLA_KV_LORA].set(w_vb)
    wkv = jnp.concatenate([wk.reshape(MLA_ROW_W, -1), wv.reshape(MLA_ROW_W, -1)], axis=1)
    wabs = jnp.zeros((MLA_H, MLA_QK, MLA_ROW_W), w_kb.dtype)
    wabs = wabs.at[:, :MLA_NOPE, :MLA_KV_LORA].set(jnp.transpose(w_kb, (1, 2, 0)))
    wabs = wabs.at[:, MLA_NOPE:MLA_NOPE + MLA_ROPE, MLA_KV_LORA:MLA_KV_LORA + MLA_ROPE].set(
        jnp.broadcast_to(eye[None], (MLA_H, MLA_ROPE, MLA_ROPE)))
    wvt = jnp.transpose(w_vb, (1, 0, 2))
    return (wc.astype(BF16), wq.astype(BF16), wq_rot.astype(BF16), wkv.astype(BF16), wabs.astype(BF16),
            wvt.astype(BF16))


def _mla_rope_tables(pos):
    half = MLA_ROPE // 2
    inv = ROPE_THETA ** (-jnp.arange(half, dtype=F32) / half)
    ang = pos.astype(F32)[:, None] * inv[None, :]
    cos, sin = jnp.cos(ang), jnp.sin(ang)
    n = pos.shape[0]
    cos2, sin2 = jnp.concatenate([cos, cos], -1), jnp.concatenate([sin, sin], -1)
    z = lambda w: jnp.zeros((n, w), F32)
    cos_k = jnp.concatenate([cos2, z(LANES - MLA_ROPE)], -1)
    sin_k = jnp.concatenate([sin2, z(LANES - MLA_ROPE)], -1)
    cos_q = jnp.concatenate([jnp.ones((n, MLA_NOPE), F32), cos2, z(MLA_QK - MLA_NOPE - MLA_ROPE)], -1)
    sin_q = jnp.concatenate([z(MLA_NOPE), sin2, z(MLA_QK - MLA_NOPE - MLA_ROPE)], -1)
    return cos_k, sin_k, cos_q, sin_q


def _rms_rows(x, g):
    return x * lax.rsqrt(jnp.mean(x * x, -1, keepdims=True) + RMS_EPS) * g


def _mla_in_kernel(x_ref, w_ref, gq_ref, gkv_ref, cos_ref, sin_ref, cq_ref, rows_ref):
    xb = x_ref[...].astype(BF16)
    h = jnp.concatenate([jnp.dot(xb, w_ref[:, c:c + 2 * MXU_N], preferred_element_type=F32)
                         for c in range(0, MLA_CW, 2 * MXU_N)], axis=1)
    cq_ref[...] = _rms_rows(h[:, :MLA_Q_LORA], gq_ref[...]).astype(cq_ref.dtype)
    rows_ref[:, :MLA_KV_LORA] = _rms_rows(h[:, MLA_Q_LORA:MLA_Q_LORA + MLA_KV_LORA], gkv_ref[...])
    rows_ref[:, MLA_KV_LORA:] = (h[:, MLA_KR:MLA_KR + LANES] * cos_ref[...]
                                 + h[:, MLA_KR_ROT:MLA_KR_ROT + LANES] * sin_ref[...])


def _pos_block(n_prompt_tiles, tiles_per_seq):
    return lambda i: (jnp.where(i < n_prompt_tiles, i % tiles_per_seq, tiles_per_seq + i - n_prompt_tiles), 0)


def _mla_in(x, wc, gq, gkv, cos_k, sin_k, n_prompt_tiles, tiles_per_seq, tile=TOK_TILE):
    n, d = x.shape
    pos = pl.BlockSpec((tile, LANES), _pos_block(n_prompt_tiles, tiles_per_seq))
    vec = lambda w: pl.BlockSpec((1, w), lambda i: (0, 0))
    return pl.pallas_call(
        _mla_in_kernel, grid=(n // tile,),
        in_specs=[pl.BlockSpec((tile, d), lambda i: (i, 0)), pl.BlockSpec((d, MLA_CW), lambda i: (0, 0)),
                  vec(MLA_Q_LORA), vec(MLA_KV_LORA), pos, pos],
        out_specs=[pl.BlockSpec((tile, MLA_Q_LORA), lambda i: (i, 0)), pl.BlockSpec((tile, MLA_ROW_W), lambda i: (i, 0))],
        out_shape=[jax.ShapeDtypeStruct((n, MLA_Q_LORA), BF16), jax.ShapeDtypeStruct((n, MLA_ROW_W), F32)],
        compiler_params=_cparams(1), name="mla_in")(
            x, wc, gq.reshape(1, -1), gkv.reshape(1, -1), cos_k, sin_k)


def _mla_q_kernel(c_ref, w_ref, wrot_ref, cos_ref, sin_ref, q_ref):
    c = c_ref[...]
    cos, sin = cos_ref[...], sin_ref[...]
    for h in range(MLA_H):
        cols = slice(h * MLA_QK, (h + 1) * MLA_QK)
        q = jnp.dot(c, w_ref[:, cols], preferred_element_type=F32)
        qr = jnp.dot(c, wrot_ref[:, cols], preferred_element_type=F32)
        q_ref[:, cols] = (q * cos + qr * sin).astype(q_ref.dtype)


def _mla_q(cq, wq, wq_rot, cos_q, sin_q, n_prompt_tiles, tiles_per_seq, tile=TOK_TILE):
    n, k = cq.shape
    m = wq.shape[1]
    pos = pl.BlockSpec((tile, LANES), _pos_block(n_prompt_tiles, tiles_per_seq))
    w = pl.BlockSpec((k, m), lambda i: (0, 0))
    return pl.pallas_call(
        _mla_q_kernel, grid=(n // tile,),
        in_specs=[pl.BlockSpec((tile, k), lambda i: (i, 0)), w, w, pos, pos],
        out_specs=pl.BlockSpec((tile, m), lambda i: (i, 0)),
        out_shape=jax.ShapeDtypeStruct((n, m), BF16),
        compiler_params=_cparams(1), name="mla_q")(cq, wq, wq_rot, cos_q, sin_q)


MLA_ATT_TILE = 256


def _mla_prompt_kernel(q_ref, k_ref, v_ref, o_ref):
    tq = q_ref.shape[0]
    qi = pl.program_id(2)
    scale = (MLA_NOPE + MLA_ROPE) ** -0.5
    lane = lax.broadcasted_iota(I32, (tq, LANES), 1)
    diag = lax.broadcasted_iota(I32, (tq, tq), 1) <= lax.broadcasted_iota(I32, (tq, tq), 0)
    out = None
    for hh in range(2):
        q = q_ref[:, hh * MLA_QK:(hh + 1) * MLA_QK]

        def step(kc, carry, masked):
            m, l, acc = carry
            r = pl.ds(pl.multiple_of(kc * tq, tq), tq)
            s = _dot_nt(q, k_ref[r, hh * MLA_QK:(hh + 1) * MLA_QK]) * scale
            if masked:
                s = jnp.where(diag, s, -jnp.inf)
            m_new = jnp.maximum(m, jnp.max(s, -1, keepdims=True))
            a = jnp.exp(m - m_new)
            e = jnp.exp(s - m_new)
            l = a * l + jnp.sum(e, -1, keepdims=True)
            acc = a * acc + jnp.dot(e.astype(BF16), v_ref[r, :], preferred_element_type=F32)
            return m_new, l, acc

        init = (jnp.full((tq, 1), -jnp.inf, F32), jnp.zeros((tq, 1), F32), jnp.zeros((tq, LANES), F32))
        carry = lax.fori_loop(0, qi, lambda kc, c: step(kc, c, False), init)
        m, l, acc = step(qi, carry, True)
        o = acc * (1.0 / l)
        out = o if out is None else jnp.where(lane < MLA_V, out, o)
    o_ref[...] = out.astype(o_ref.dtype)


def _mla_prompt(q, kv, batch, seq):
    tq = min(MLA_ATT_TILE, seq)
    nq = seq // tq
    k_cols = MLA_H * MLA_QK // (2 * MLA_QK)
    return pl.pallas_call(
        _mla_prompt_kernel, grid=(batch, MLA_H // 2, nq),
        in_specs=[pl.BlockSpec((tq, 2 * MLA_QK), lambda b, j, i: (b * nq + i, j)),
                  pl.BlockSpec((seq, 2 * MLA_QK), lambda b, j, i: (b, j)),
                  pl.BlockSpec((seq, 2 * MLA_V), lambda b, j, i: (b, 2 * k_cols + j))],
        out_specs=pl.BlockSpec((tq, 2 * MLA_V), lambda b, j, i: (b * nq + i, j)),
        out_shape=jax.ShapeDtypeStruct((batch * seq, MLA_H * MLA_V), BF16),
        compiler_params=_cparams(3), name="mla_prompt")(q, kv, kv)


def _mla_absorb_kernel(q_ref, w_ref, o_ref):
    o_ref[0] = jnp.dot(q_ref[...], w_ref[0], preferred_element_type=F32)


def _mla_absorb(qs, wabs):
    batch = qs.shape[0]
    return pl.pallas_call(
        _mla_absorb_kernel, grid=(MLA_H,),
        in_specs=[pl.BlockSpec((batch, MLA_QK), lambda h: (0, h)), pl.BlockSpec((1, MLA_QK, MLA_ROW_W), lambda h: (h, 0, 0))],
        out_specs=pl.BlockSpec((1, batch, MLA_ROW_W), lambda h: (h, 0, 0)),
        out_shape=jax.ShapeDtypeStruct((MLA_H, batch, MLA_ROW_W), F32),
        compiler_params=_cparams(1), name="mla_absorb")(qs, wabs)


def _mla_sample_kernel(tbl_ref, cache_ref, q_ref, new_ref, o_ref, buf, sem, *, n_pages, past):
    b = pl.program_id(0)
    nb = pl.num_programs(0)
    slot = b % 2
    width = MLA_KV_LORA + MLA_ROPE

    def page(bb, p, sl):
        return pltpu.make_async_copy(cache_ref.at[tbl_ref[bb * n_pages + p]], buf.at[sl, pl.ds(p * PAGE, PAGE)],
                                     sem.at[sl])

    def fetch(bb, sl):
        def body(p, c):
            page(bb, p, sl).start()
            return c
        lax.fori_loop(0, n_pages, body, 0)

    @pl.when(b == 0)
    def _():
        fetch(b, slot)

    @pl.when(b + 1 < nb)
    def _():
        fetch(b + 1, 1 - slot)

    def wait(p, c):
        page(b, 0, slot).wait()
        return c
    lax.fori_loop(0, n_pages, wait, 0)

    scale = (MLA_NOPE + MLA_ROPE) ** -0.5
    q = q_ref[0]
    rows = buf[slot].astype(BF16)
    s = _dot_nt(q[:, :width].astype(BF16), rows) * scale
    new = new_ref[0]
    s_new = jnp.sum(q * new, axis=-1, keepdims=True) * scale
    m = jnp.maximum(jnp.max(s, -1, keepdims=True), s_new)
    e, e_new = jnp.exp(s - m), jnp.exp(s_new - m)
    den = jnp.sum(e, -1, keepdims=True) + e_new
    o = jnp.dot(e.astype(BF16), rows[:, :MLA_KV_LORA], preferred_element_type=F32) + e_new * new[:, :MLA_KV_LORA]
    o_ref[0] = o * (1.0 / den)


def _mla_sample(table, cache, qabs, new_rows, past):
    batch, n_pages = table.shape
    width = cache.shape[-1]
    return pl.pallas_call(
        functools.partial(_mla_sample_kernel, n_pages=n_pages, past=past),
        grid_spec=pltpu.PrefetchScalarGridSpec(
            num_scalar_prefetch=1, grid=(batch,),
            in_specs=[pl.BlockSpec(memory_space=pl.ANY), pl.BlockSpec((1, MLA_H, MLA_ROW_W), lambda b, t: (b, 0, 0)),
                      pl.BlockSpec((1, 1, MLA_ROW_W), lambda b, t: (b, 0, 0))],
            out_specs=pl.BlockSpec((1, MLA_H, MLA_KV_LORA), lambda b, t: (b, 0, 0)),
            scratch_shapes=[pltpu.VMEM((2, past, width), F32), pltpu.SemaphoreType.DMA((2,))]),
        out_shape=jax.ShapeDtypeStruct((batch, MLA_H, MLA_KV_LORA), F32),
        compiler_params=_cparams(1), name="mla_sample")(table.reshape(-1), cache, qabs, new_rows)


def _mla_vup_kernel(o_ref, w_ref, y_ref):
    y_ref[0] = jnp.dot(o_ref[0].astype(BF16), w_ref[0], preferred_element_type=F32)


def _mla_vup(o_lat, wvt):
    _, batch, lat = o_lat.shape
    return pl.pallas_call(
        _mla_vup_kernel, grid=(MLA_H,),
        in_specs=[pl.BlockSpec((1, batch, lat), lambda h: (h, 0, 0)), pl.BlockSpec((1, lat, MLA_V), lambda h: (h, 0, 0))],
        out_specs=pl.BlockSpec((1, batch, MLA_V), lambda h: (h, 0, 0)),
        out_shape=jax.ShapeDtypeStruct((MLA_H, batch, MLA_V), F32),
        compiler_params=_cparams(1), name="mla_vup")(o_lat, wvt)


def _mixer_c(x, n_prompt, batch_p, seq, batch_s, cache, table, w_in, gq, gkv, w_qb, w_kb, w_vb):
    n = x.shape[0]
    past = table.shape[1] * PAGE
    wc, wq, wq_rot, wkv, wabs, wvt = _prep_mla(w_in, w_qb, w_kb, w_vb)
    pos = jnp.concatenate([jnp.arange(seq), jnp.full((n - n_prompt,), past)])
    cos_k, sin_k, cos_q, sin_q = _mla_rope_tables(pos)
    n_pt, tps = n_prompt // TOK_TILE, seq // TOK_TILE
    cq, rows = _mla_in(x, wc, gq, gkv, cos_k, sin_k, n_pt, tps)
    q = _mla_q(cq, wq, wq_rot, cos_q, sin_q, n_pt, tps)
    kv = _mm(rows, wkv, out_dtype=BF16)
    attn_p = _mla_prompt(q, kv, batch_p, seq)
    qabs = _mla_absorb(q[n_prompt:n_prompt + batch_s], wabs)
    new_rows = rows[n_prompt:n_prompt + batch_s].reshape(batch_s, 1, MLA_ROW_W)
    o_lat = _mla_sample(table, cache, jnp.transpose(qabs, (1, 0, 2)), new_rows, past)
    o_s = _mla_vup(jnp.transpose(o_lat, (1, 0, 2)), wvt)
    o_s = jnp.transpose(o_s, (1, 0, 2)).reshape(batch_s, MLA_H * MLA_V).astype(BF16)
    attn = jnp.concatenate([attn_p, o_s, jnp.zeros((n - n_prompt - batch_s, MLA_H * MLA_V), BF16)], axis=0)
    return attn, rows


def _dev_mixer_c(xp, xs, cache, table, w_in, gq, gkv, w_qb, w_kb, w_vb, w_out):
    bp, seq, d = xp.shape
    bs = xs.shape[0]
    n_prompt = bp * seq
    x = jnp.concatenate([xp.reshape(n_prompt, d), xs.reshape(bs, d), jnp.zeros((TOK_TILE - bs, d), F32)], axis=0)
    attn, rows = _mixer_c(x, n_prompt, bp, seq, bs, cache, table, w_in, gq, gkv, w_qb, w_kb, w_vb)
    mix = _mm(attn, w_out.astype(BF16))
    width = MLA_KV_LORA + MLA_ROPE
    return ((mix[:n_prompt].reshape(bp, seq, d), rows[:n_prompt, :width].reshape(bp, seq, width)),
            (mix[n_prompt:n_prompt + bs].reshape(bs, 1, d), rows[n_prompt:n_prompt + bs, :width].reshape(bs, 1, width)))


def kernel(x_prompt, x_sample, cache_nsa_kv, state_nsa_win, state_ret, cache_mla, page_table, w_in_ab, w_out_ab, w_cmp1, w_cmp2, cmp_pe, w_in_mla, mla_q_norm, mla_kv_norm, w_q_up, w_k_up, w_v_up, w_out_mla, ln_mix_g, ln_mix_b, ln_ffn_g, ln_ffn_b, w_router, router_bias, w_exp_gate, w_exp_up, w_exp_down):
    bp, seq, d = x_prompt.shape
    bs = x_sample.shape[0]
    assert x_sample.shape[1] == 1 and bs <= TOK_TILE and seq % TOK_TILE == 0 and ln_mix_g.shape[0] == DEPTH
    n_prompt = bp * seq
    pad_rows = TOK_TILE - bs
    x = jnp.concatenate([x_prompt.reshape(n_prompt, d), x_sample.reshape(bs, d), jnp.zeros((pad_rows, d), F32)], axis=0)
    sample = slice(n_prompt, n_prompt + bs)
    mla_w = MLA_KV_LORA + MLA_ROPE
    kv_p, kv_s, win_p, win_s, ret_p, ret_s, mla_p, mla_s = [], [], [], [], [], [], [], []
    for layer in range(DEPTH):
        i = layer // 2
        if layer % 2 == 0:
            h = _mm(x, _prep_w_in_ab(w_in_ab[i]))
            w1bd, w2bd, pe_l = _prep_compress(w_cmp1[i], w_cmp2[i], cmp_pe[i])
            kvc = _compress_prompt(h, bp, seq, w1bd, w2bd, pe_l)
            o_nsa = _nsa_prompt(h, kvc, bp, seq)
            o_ret, s_p = _ret_prompt(h, bp, seq)
            hs = h[sample]
            mixed_s, w_s, s_s = _mixer_ab_sample(hs, cache_nsa_kv[i], state_nsa_win[i], state_ret[i], page_table,
                                                 w1bd, w2bd, pe_l)
            mixed = jnp.concatenate([jnp.concatenate([o_nsa, o_ret], axis=1), mixed_s,
                                     jnp.zeros((pad_rows, mixed_s.shape[1]), F32)], axis=0)
            w_out = w_out_ab[i]
            kv_p.append(h[:n_prompt, AB_KV:AB_KV + 512].reshape(bp, seq, 4, NSA_G, NSA_HD))
            kv_s.append(hs[:, AB_KV:AB_KV + 512].reshape(bs, 1, 4, NSA_G, NSA_HD))
            n_keep = min(NSA_WINDOW, seq)
            win_p.append(h[:n_prompt, AB_KV + 512:AB_KV + 768].reshape(bp, seq, 2, NSA_G, NSA_HD)[:, seq - n_keep:])
            win_s.append(w_s)
            ret_p.append(s_p)
            ret_s.append(s_s)
        else:
            mixed, rows = _mixer_c(x, n_prompt, bp, seq, bs, cache_mla[i], page_table, w_in_mla[i], mla_q_norm[i],
                                   mla_kv_norm[i], w_q_up[i], w_k_up[i], w_v_up[i])
            w_out = w_out_mla[i]
            mla_p.append(rows[:n_prompt, :mla_w].reshape(bp, seq, mla_w))
            mla_s.append(rows[sample, :mla_w].reshape(bs, 1, mla_w))
        x = _proj_ln(mixed, w_out.astype(BF16), x, ln_mix_g[layer], ln_mix_b[layer])
        x = _moe_ln(x, w_router, router_bias, w_exp_gate[layer].astype(BF16), w_exp_up[layer].astype(BF16),
                    w_exp_down[layer].astype(BF16), ln_ffn_g[layer], ln_ffn_b[layer])
    return (x[:n_prompt].reshape(bp, seq, d), x[sample].reshape(bs, 1, d), jnp.stack(kv_p), jnp.stack(kv_s),
            jnp.stack(win_p), jnp.stack(win_s), jnp.stack(ret_p), jnp.stack(ret_s), jnp.stack(mla_p), jnp.stack(mla_s))
```

```python
import functools
import math

import numpy as np
import jax
import jax.numpy as jnp
from jax import lax
from jax.experimental import pallas as pl
from jax.experimental.pallas import tpu as pltpu

F32 = jnp.float32
BF16 = jnp.bfloat16
I32 = jnp.int32

PAGE = 128
NSA_H, NSA_G, NSA_HD = 8, 2, 64
NSA_HPG = NSA_H // NSA_G
CMP_BLOCK, CMP_STRIDE, CMP_HIDDEN = 32, 16, 64
SEL_BLOCK, SEL_TOPN, SEL_LOCAL = 64, 8, 2
NSA_WINDOW = 512
FORCE_SCORE = 1.0e6
RET_H, RET_DK, RET_DV, RET_CHUNK = 4, 128, 128, 128
MLA_H, MLA_Q_LORA, MLA_KV_LORA, MLA_NOPE, MLA_ROPE, MLA_V = 16, 384, 256, 64, 32, 64
ROPE_THETA = 10000.0
N_EXPERTS, N_GROUPS, EPG, D_EXPERT = 16, 4, 4, 512
N_PAIRS = EPG * (EPG - 1) // 2
N_CLASSES = N_GROUPS * N_PAIRS
Q_BLOCK = 128
LN_EPS = 1e-5
RMS_EPS = 1e-6
DEPTH = 2
ALPHA = (2 * DEPTH) ** 0.25

LANES = 128
MXU_N = 256
VMEM_LIMIT = 48 * 1024 * 1024

TOK_TILE = 512
MOE_TILE = 256
CLS_PAD = 32

AB_Q, AB_QR, AB_KR, AB_VR, AB_GR, AB_KV, AB_GATE = 0, 512, 1024, 1536, 2048, 2560, 3328
AB_W = 3584


def _cparams(n_axes):
    return pltpu.CompilerParams(dimension_semantics=("arbitrary",) * n_axes, vmem_limit_bytes=VMEM_LIMIT)


def _split_bf16(w):
    hi = w.astype(BF16)
    return hi, (w - hi.astype(F32)).astype(BF16)


def _dot_split(x, w_hi, w_lo):
    x_hi, x_lo = _split_bf16(x)
    dot = functools.partial(jnp.dot, preferred_element_type=F32)
    return dot(x_hi, w_hi) + (dot(x_lo, w_hi) + dot(x_hi, w_lo))


def _mm_kernel(x_ref, w_ref, *rest, precise_from):
    o_ref = rest[-1]
    step = 2 * MXU_N

    def single_pass():
        xb = x_ref[...].astype(BF16)
        for c in range(0, o_ref.shape[1], step):
            o_ref[:, c:c + step] = jnp.dot(xb, w_ref[:, c:c + step], preferred_element_type=F32).astype(o_ref.dtype)

    if precise_from is None:
        single_pass()
        return
    pl.when(pl.program_id(0) < precise_from)(single_pass)

    @pl.when(pl.program_id(0) >= precise_from)
    def _():
        x = x_ref[...]
        for c in range(0, o_ref.shape[1], step):
            o_ref[:, c:c + step] = _dot_split(x, w_ref[:, c:c + step], rest[0][:, c:c + step]).astype(o_ref.dtype)


def _mm(x, w, w_lo=None, precise_from=None, out_dtype=F32, tile=TOK_TILE):
    n, k = x.shape
    m = w.shape[1]
    assert n % tile == 0 and m % (2 * MXU_N) == 0
    weights = (w,) if precise_from is None else (w, w_lo)
    return pl.pallas_call(
        functools.partial(_mm_kernel, precise_from=precise_from), grid=(n // tile,),
        in_specs=[pl.BlockSpec((tile, k), lambda i: (i, 0))] + [pl.BlockSpec((k, m), lambda i: (0, 0))] * len(weights),
        out_specs=pl.BlockSpec((tile, m), lambda i: (i, 0)),
        out_shape=jax.ShapeDtypeStruct((n, m), out_dtype),
        compiler_params=_cparams(1), name="mm")(x, *weights)


def _layer_norm_rows(z, g, b):
    mu = jnp.mean(z, -1, keepdims=True)
    zc = z - mu
    var = jnp.mean(zc * zc, -1, keepdims=True)
    return zc * lax.rsqrt(var + LN_EPS) * g + b


def _proj_ln_kernel(a_ref, w_ref, *rest, precise_from):
    x_ref, g_ref, b_ref, o_ref = rest[-4:]

    def finish(y):
        o_ref[...] = _layer_norm_rows(ALPHA * x_ref[...] + y, g_ref[...], b_ref[...])

    def single_pass():
        finish(jnp.dot(a_ref[...].astype(BF16), w_ref[...], preferred_element_type=F32))

    if precise_from is None:
        single_pass()
        return
    pl.when(pl.program_id(0) < precise_from)(single_pass)

    @pl.when(pl.program_id(0) >= precise_from)
    def _():
        finish(_dot_split(a_ref[...], w_ref[...], rest[0][...]))


def _proj_ln(a, w, x, g, b, w_lo=None, precise_from=None, tile=TOK_TILE):
    n, k = a.shape
    d = w.shape[1]
    weights = (w,) if precise_from is None else (w, w_lo)
    row = pl.BlockSpec((1, d), lambda i: (0, 0))
    return pl.pallas_call(
        functools.partial(_proj_ln_kernel, precise_from=precise_from), grid=(n // tile,),
        in_specs=[pl.BlockSpec((tile, k), lambda i: (i, 0))] + [pl.BlockSpec((k, d), lambda i: (0, 0))] * len(weights)
        + [pl.BlockSpec((tile, d), lambda i: (i, 0)), row, row],
        out_specs=pl.BlockSpec((tile, d), lambda i: (i, 0)),
        out_shape=jax.ShapeDtypeStruct((n, d), F32),
        compiler_params=_cparams(1), name="proj_ln")(a, *weights, x, g.reshape(1, d), b.reshape(1, d))


def _router_kernel(x_ref, wrt_ref, wrt_lo_ref, bias_ref, tri_ref, cls_ref, rank_ref, cnt_ref, carry_ref):
    @pl.when(pl.program_id(0) == 0)
    def _():
        carry_ref[...] = jnp.zeros_like(carry_ref)

    tile = x_ref.shape[0]
    x_hi, x_lo = _split_bf16(x_ref[...])
    logits = _dot_nt(wrt_ref[...], x_hi) + (_dot_nt(wrt_ref[...], x_lo) + _dot_nt(wrt_lo_ref[...], x_hi))
    ssel = jax.nn.sigmoid(logits) + bias_ref[...]
    rows = [ssel[e:e + 1, :] for e in range(N_EXPERTS)]

    def top2sum(a, b, c, d):
        return jnp.maximum(jnp.maximum(jnp.maximum(a + b, a + c), jnp.maximum(a + d, b + c)),
                           jnp.maximum(b + d, c + d))

    gscore = [top2sum(*rows[EPG * g:EPG * (g + 1)]) for g in range(N_GROUPS)]
    best, gi = gscore[0], jnp.zeros((1, tile), I32)
    for g in range(1, N_GROUPS):
        better = gscore[g] > best
        gi = jnp.where(better, g, gi)
        best = jnp.where(better, gscore[g], best)
    v = []
    for j in range(EPG):
        vj = rows[j]
        for g in range(1, N_GROUPS):
            vj = jnp.where(gi == g, rows[EPG * g + j], vj)
        v.append(vj)
    sel = []
    for i in range(EPG):
        r = jnp.zeros((1, tile), I32)
        for j in range(EPG):
            if j == i:
                continue
            beats = (v[j] > v[i]) | ((v[j] == v[i]) if j < i else False)
            r = r + beats.astype(I32)
        sel.append(r < 2)
    lo = jnp.where(sel[0], 0, jnp.where(sel[1], 1, 2))
    hi = jnp.where(sel[3], 3, jnp.where(sel[2], 2, 1))
    base = jnp.where(lo == 0, 0, jnp.where(lo == 1, 3, 5))
    cls = gi * N_PAIRS + base + hi - lo - 1

    onehot = (lax.broadcasted_iota(I32, (CLS_PAD, tile), 0) == cls).astype(F32)
    prefix = jnp.dot(onehot.astype(BF16), tri_ref[...], preferred_element_type=F32)
    carry = carry_ref[...]
    rank = jnp.sum(onehot * (prefix - 1.0 + carry), axis=0, keepdims=True)
    carry = carry + jnp.sum(onehot, axis=1, keepdims=True)
    carry_ref[...] = carry
    cls_ref[...] = cls
    rank_ref[...] = rank.astype(I32)
    cnt_ref[...] = carry.astype(I32)


def _route(x, wrt, wrt_lo, bias, tile=TOK_TILE):
    n, d = x.shape
    tri = (np.arange(tile)[:, None] <= np.arange(tile)[None, :]).astype(np.float32)
    w_spec = pl.BlockSpec((N_EXPERTS, d), lambda i: (0, 0))
    return pl.pallas_call(
        _router_kernel, grid=(n // tile,),
        in_specs=[pl.BlockSpec((tile, d), lambda i: (i, 0)), w_spec, w_spec,
                  pl.BlockSpec((N_EXPERTS, 1), lambda i: (0, 0)), pl.BlockSpec((tile, tile), lambda i: (0, 0))],
        out_specs=[pl.BlockSpec((1, tile), lambda i: (0, i)), pl.BlockSpec((1, tile), lambda i: (0, i)),
                   pl.BlockSpec((CLS_PAD, 1), lambda i: (0, 0))],
        out_shape=[jax.ShapeDtypeStruct((1, n), I32), jax.ShapeDtypeStruct((1, n), I32),
                   jax.ShapeDtypeStruct((CLS_PAD, 1), I32)],
        scratch_shapes=[pltpu.VMEM((CLS_PAD, 1), F32)],
        compiler_params=_cparams(1), name="moe_route")(x, wrt, wrt_lo, bias, jnp.asarray(tri, BF16))


def _row_copy(src_ref, src_row, dst_ref, dst_row, sem):
    return pltpu.make_async_copy(src_ref.at[pl.ds(src_row, 1)], dst_ref.at[pl.ds(dst_row, 1)], sem)


ROW_DMA_UNROLL = 8


def _slot_kernel(off_ref, cls_ref, rank_ref, slot_ref):
    cls = cls_ref[...]
    slot = rank_ref[...]
    for c in range(N_CLASSES):
        slot = slot + jnp.where(cls == c, off_ref[c], 0)
    slot_ref[...] = slot


def _slots(cls, rank, off):
    row = pl.BlockSpec(cls.shape, lambda i, off: (0, 0))
    return pl.pallas_call(
        _slot_kernel,
        grid_spec=pltpu.PrefetchScalarGridSpec(num_scalar_prefetch=1, grid=(1,), in_specs=[row, row], out_specs=row),
        out_shape=jax.ShapeDtypeStruct(cls.shape, I32),
        compiler_params=_cparams(1), name="moe_slots")(off, cls, rank)


def _scatter_kernel(slot_ref, x_ref, xs_in_ref, xs_ref, sem):
    del xs_in_ref
    tile = x_ref.shape[0]

    def start(r, c):
        _row_copy(x_ref, r, xs_ref, slot_ref[0, r], sem).start()
        return c

    lax.fori_loop(0, tile, start, 0, unroll=ROW_DMA_UNROLL)

    def wait(r, c):
        _row_copy(x_ref, 0, xs_ref, 0, sem).wait()
        return c

    lax.fori_loop(0, tile, wait, 0, unroll=ROW_DMA_UNROLL)


def _scatter_rows(x, slot, n_slots, tile=MOE_TILE):
    n, d = x.shape
    return pl.pallas_call(
        _scatter_kernel, grid=(n // tile,),
        in_specs=[pl.BlockSpec((1, tile), lambda i: (0, i), memory_space=pltpu.SMEM),
                  pl.BlockSpec((tile, d), lambda i: (i, 0)), pl.BlockSpec(memory_space=pl.ANY)],
        out_specs=pl.BlockSpec(memory_space=pl.ANY),
        scratch_shapes=[pltpu.SemaphoreType.DMA(())],
        out_shape=jax.ShapeDtypeStruct((n_slots, d), F32),
        input_output_aliases={2: 0},
        compiler_params=_cparams(1), name="moe_scatter")(slot, x, jnp.zeros((n_slots, d), F32))


def _expert_kernel(e1_ref, e2_ref, valid_ref, xs_ref, wr_ref, g1_ref, u1_ref, d1_ref, g2_ref, u2_ref, d2_ref,
                   o_ref):
    j = pl.program_id(0)

    @pl.when(valid_ref[j] == 0)
    def _():
        o_ref[...] = jnp.zeros_like(o_ref)

    @pl.when(valid_ref[j] != 0)
    def _():
        xb = xs_ref[...].astype(BF16)
        s = jax.nn.sigmoid(jnp.dot(xb, wr_ref[...], preferred_element_type=F32))
        lane = lax.broadcasted_iota(I32, s.shape, 1)
        w1 = jnp.sum(jnp.where(lane == e1_ref[j], s, 0.0), axis=1, keepdims=True)
        w2 = jnp.sum(jnp.where(lane == e2_ref[j], s, 0.0), axis=1, keepdims=True)
        tot = w1 + w2

        def mlp(g_ref, u_ref, d_ref):
            hg = jnp.dot(xb, g_ref[0], preferred_element_type=F32)
            hu = jnp.dot(xb, u_ref[0], preferred_element_type=F32)
            hdn = hg * jax.nn.sigmoid(hg) * hu
            return jnp.dot(hdn.astype(BF16), d_ref[0], preferred_element_type=F32)

        o_ref[...] = (w1 / tot) * mlp(g1_ref, u1_ref, d1_ref) + (w2 / tot) * mlp(g2_ref, u2_ref, d2_ref)


def _expert_pairs(xs, wr, wg, wu, wd, e1, e2, valid, tile=MOE_TILE):
    n_slots, d = xs.shape
    de = wg.shape[2]
    up = lambda sel: pl.BlockSpec((1, d, de), lambda j, e1, e2, v: ((e1, e2)[sel][j], 0, 0))
    down = lambda sel: pl.BlockSpec((1, de, d), lambda j, e1, e2, v: ((e1, e2)[sel][j], 0, 0))
    return pl.pallas_call(
        _expert_kernel,
        grid_spec=pltpu.PrefetchScalarGridSpec(
            num_scalar_prefetch=3, grid=(n_slots // tile,),
            in_specs=[pl.BlockSpec((tile, d), lambda j, e1, e2, v: (j, 0)),
                      pl.BlockSpec((d, N_EXPERTS), lambda j, e1, e2, v: (0, 0)),
                      up(0), up(0), down(0), up(1), up(1), down(1)],
            out_specs=pl.BlockSpec((tile, d), lambda j, e1, e2, v: (j, 0))),
        out_shape=jax.ShapeDtypeStruct((n_slots, d), F32),
        compiler_params=_cparams(1), name="moe_experts")(e1, e2, valid, xs, wr, wg, wu, wd, wg, wu, wd)


def _gather_ln_kernel(slot_ref, next_ref, x_ref, ys_ref, g_ref, b_ref, o_ref, ybuf, sem):
    tile = x_ref.shape[0]
    i = pl.program_id(0)
    cur = i % 2

    def fetch(idx_ref, buf_slot):
        def start(r, c):
            _row_copy(ys_ref, idx_ref[0, r], ybuf.at[buf_slot], r, sem.at[buf_slot]).start()
            return c
        lax.fori_loop(0, tile, start, 0, unroll=ROW_DMA_UNROLL)

    @pl.when(i == 0)
    def _():
        fetch(slot_ref, cur)

    @pl.when(i + 1 < pl.num_programs(0))
    def _():
        fetch(next_ref, 1 - cur)

    def wait(r, c):
        _row_copy(ys_ref, 0, ybuf.at[cur], 0, sem.at[cur]).wait()
        return c

    lax.fori_loop(0, tile, wait, 0, unroll=ROW_DMA_UNROLL)
    o_ref[...] = _layer_norm_rows(ALPHA * x_ref[...] + ybuf[cur], g_ref[...], b_ref[...])


def _gather_ln(x, ys, slot, g, b, tile=MOE_TILE):
    n, d = x.shape
    n_tiles = n // tile
    vec = pl.BlockSpec((1, d), lambda i: (0, 0))
    return pl.pallas_call(
        _gather_ln_kernel, grid=(n_tiles,),
        in_specs=[pl.BlockSpec((1, tile), lambda i: (0, i), memory_space=pltpu.SMEM),
                  pl.BlockSpec((1, tile), lambda i: (0, jnp.minimum(i + 1, n_tiles - 1)), memory_space=pltpu.SMEM),
                  pl.BlockSpec((tile, d), lambda i: (i, 0)), pl.BlockSpec(memory_space=pl.ANY), vec, vec],
        out_specs=pl.BlockSpec((tile, d), lambda i: (i, 0)),
        scratch_shapes=[pltpu.VMEM((2, tile, d), F32), pltpu.SemaphoreType.DMA((2,))],
        out_shape=jax.ShapeDtypeStruct((n, d), F32),
        compiler_params=_cparams(1), name="moe_gather_ln")(slot, slot, x, ys, g.reshape(1, d), b.reshape(1, d))


_PAIR_LO = np.array([0, 0, 0, 1, 1, 2], np.int32)
_PAIR_HI = np.array([1, 2, 3, 2, 3, 3], np.int32)


def _moe_ln(x, w_router, router_bias, wg, wu, wd, g, b):
    n, d = x.shape
    cls, rank, cnt = _route(x, *_split_bf16(w_router.T), router_bias.reshape(N_EXPERTS, 1).astype(F32))
    cnt = cnt[:N_CLASSES, 0]
    tiles = (cnt + MOE_TILE - 1) // MOE_TILE
    tile_end = jnp.cumsum(tiles)
    off = jnp.zeros((CLS_PAD,), I32).at[:N_CLASSES].set((tile_end - tiles) * MOE_TILE)
    n_tiles = n // MOE_TILE + N_CLASSES
    tile_id = jnp.arange(n_tiles, dtype=I32)
    tile_cls = jnp.minimum(jnp.sum(tile_id[:, None] >= tile_end[None, :], axis=1), N_CLASSES - 1).astype(I32)
    valid = (tile_id < tile_end[-1]).astype(I32)
    grp, pair = tile_cls // N_PAIRS, tile_cls % N_PAIRS
    e1 = grp * EPG + jnp.asarray(_PAIR_LO)[pair]
    e2 = grp * EPG + jnp.asarray(_PAIR_HI)[pair]
    slot = _slots(cls, rank, off)
    xs = _scatter_rows(x, slot, n_tiles * MOE_TILE)
    ys = _expert_pairs(xs, w_router.astype(BF16), wg, wu, wd, e1, e2, valid)
    return _gather_ln(x, ys, slot, g, b)


def _prep_w_in_ab(w):
    k = w.shape[0]
    q, kv, gate, qr, kr, vr, gr = jnp.split(w, np.cumsum([512, 768, 24, 512, 512, 512]).tolist(), axis=1)
    pad = jnp.zeros((k, AB_W - AB_GATE - 24), w.dtype)
    return _split_bf16(jnp.concatenate([q, qr, kr, vr, gr, kv, gate, pad], axis=1))


def _prep_compress(w1, w2, pe):
    hd = NSA_HD
    w1r = w1.reshape(2, 2, CMP_STRIDE, hd, CMP_HIDDEN)
    w1bd = jnp.zeros((2, 2, CMP_STRIDE, NSA_G * hd, NSA_G * CMP_HIDDEN), F32)
    w2bd = jnp.zeros((2, NSA_G * CMP_HIDDEN, NSA_G * hd), F32)
    for g in range(NSA_G):
        w1bd = w1bd.at[..., g * hd:(g + 1) * hd, g * CMP_HIDDEN:(g + 1) * CMP_HIDDEN].set(w1r)
        w2bd = w2bd.at[:, g * CMP_HIDDEN:(g + 1) * CMP_HIDDEN, g * hd:(g + 1) * hd].set(w2)
    per = pe.reshape(2, 2, CMP_STRIDE, hd)
    pe_l = jnp.concatenate([per] * NSA_G, axis=-1)
    (w1_hi, w1_lo), (w2_hi, w2_lo) = _split_bf16(w1bd), _split_bf16(w2bd)
    return w1_hi, w2_hi, pe_l, w1_lo, w2_lo


def _compress_rows(x_ref, kind, n_chunks, pe_ref, w1_ref, w2_ref, w1_lo_ref=None, w2_lo_ref=None):
    precise = w1_lo_ref is not None
    acc = [jnp.zeros((n_chunks, NSA_G * CMP_HIDDEN), F32) for _ in range(2)]
    for s in range(CMP_STRIDE):
        xs = x_ref[pl.ds(s, n_chunks, stride=CMP_STRIDE), :]
        for j in range(2):
            xp = xs + pe_ref[kind, j, s:s + 1, :]
            if precise:
                acc[j] = acc[j] + _dot_split(xp, w1_ref[kind, j, s], w1_lo_ref[kind, j, s])
            else:
                acc[j] = acc[j] + jnp.dot(xp.astype(BF16), w1_ref[kind, j, s], preferred_element_type=F32)
    hdn = jax.nn.gelu(acc[0] + pltpu.roll(acc[1], n_chunks - 1, 0))
    if precise:
        return _dot_split(hdn, w2_ref[kind], w2_lo_ref[kind])
    return jnp.dot(hdn.astype(BF16), w2_ref[kind], preferred_element_type=F32)


def _dot_split2(a, b, nt=False):
    (a_hi, a_lo), (b_hi, b_lo) = _split_bf16(a), _split_bf16(b)
    dot = _dot_nt if nt else functools.partial(jnp.dot, preferred_element_type=F32)
    return dot(a_hi, b_hi) + (dot(a_lo, b_hi) + dot(a_hi, b_lo))


def _compress_kernel(k_ref, v_ref, pe_ref, w1_ref, w2_ref, o_ref):
    n_chunks = o_ref.shape[1]
    o_ref[0, :, 0:LANES] = _compress_rows(k_ref, 0, n_chunks, pe_ref, w1_ref, w2_ref)
    o_ref[0, :, LANES:2 * LANES] = _compress_rows(v_ref, 1, n_chunks, pe_ref, w1_ref, w2_ref)


def _compress_prompt(h, batch, seq, w1bd, w2bd, pe_l):
    n_chunks = seq // CMP_STRIDE
    full = lambda a: pl.BlockSpec(a.shape, lambda b: (0,) * a.ndim)
    return pl.pallas_call(
        _compress_kernel, grid=(batch,),
        in_specs=[pl.BlockSpec((seq, LANES), lambda b: (b, AB_KV // LANES)),
                  pl.BlockSpec((seq, LANES), lambda b: (b, AB_KV // LANES + 1)), full(pe_l), full(w1bd), full(w2bd)],
        out_specs=pl.BlockSpec((1, n_chunks, 256), lambda b: (b, 0, 0)),
        out_shape=jax.ShapeDtypeStruct((batch, n_chunks, 256), F32),
        compiler_params=_cparams(1), name="nsa_compress")(h, h, pe_l, w1bd, w2bd)


def _masked_softmax(s, mask):
    s = jnp.where(mask, s, -jnp.inf)
    m = jnp.max(s, -1, keepdims=True)
    m = jnp.where(m == -jnp.inf, 0.0, m)
    e = jnp.exp(s - m)
    den = jnp.maximum(jnp.sum(e, -1, keepdims=True), jnp.finfo(F32).tiny)
    return e * (1.0 / den)


def _dot_nt(a, b):
    return lax.dot_general(a, b, (((1,), (1,)), ((), ())), preferred_element_type=F32)


def _topn_rows(imp_t, n_top):
    n_blocks = imp_t.shape[0]
    blk = lax.broadcasted_iota(I32, imp_t.shape, 0)
    rank = jnp.zeros(imp_t.shape, I32)
    for j in range(n_blocks):
        row = imp_t[j:j + 1, :]
        rank = rank + ((row > imp_t) | ((row == imp_t) & (blk > j))).astype(I32)
    return rank < n_top


NSA_SEL_CHUNK = 256


def _nsa_prompt_kernel(q_ref, gate_ref, slc_ref, win_ref, kvc_ref, selmap_ref, o_ref, *, seq, win_span):
    qb = q_ref.shape[0]
    i0 = pl.program_id(1) * qb
    n_cmp_pad = kvc_ref.shape[1]
    n_slc = seq // SEL_BLOCK
    ck = min(NSA_SEL_CHUNK, seq)
    scale = NSA_HD ** -0.5
    qpos = i0 + lax.broadcasted_iota(I32, (qb, 1), 0)
    lane = lax.broadcasted_iota(I32, (qb, LANES), 1)
    gates = jax.nn.sigmoid(gate_ref[:, 0:LANES])

    kvc = kvc_ref[0].astype(BF16)
    cmp_last = lax.broadcasted_iota(I32, (1, n_cmp_pad), 1) * CMP_STRIDE + (CMP_BLOCK - 1)
    cmp_mask = cmp_last <= qpos
    w0 = jnp.maximum(i0 + qb - win_span, 0)
    w0 = pl.multiple_of(w0, qb)
    wpos = w0 + lax.broadcasted_iota(I32, (1, win_span), 1)
    wmask = (wpos <= qpos) & (wpos > qpos - NSA_WINDOW)
    k_win = win_ref[pl.ds(w0, win_span), 0:LANES].astype(BF16)
    v_win = win_ref[pl.ds(w0, win_span), LANES:2 * LANES].astype(BF16)
    blk_t = lax.broadcasted_iota(I32, (n_slc, qb), 0)
    qpos_t = i0 + lax.broadcasted_iota(I32, (n_slc, qb), 1)
    cur_t = qpos_t // SEL_BLOCK
    forced_t = (blk_t == 0) | ((blk_t <= cur_t) & (blk_t > cur_t - SEL_LOCAL))
    valid_t = blk_t * SEL_BLOCK <= qpos_t
    chunk_blk = lax.broadcasted_iota(I32, (n_slc, ck), 1) // SEL_BLOCK
    chunk_row = lax.broadcasted_iota(I32, (n_slc, ck), 0)
    chunk_pos = lax.broadcasted_iota(I32, (1, ck), 1)
    n_chunks = (i0 + qb - 1) // ck + 1

    out_tiles = [None] * (NSA_H // 2)
    for g in range(NSA_G):
        heads = list(range(g * NSA_HPG, (g + 1) * NSA_HPG))
        q128, p_cmp = {}, {}
        imp_t = jnp.zeros((n_slc, qb), F32)
        for h in heads:
            t = q_ref[:, (h // 2) * LANES:(h // 2 + 1) * LANES] * scale
            t = jnp.where((lane // NSA_HD) == (h % 2), t, 0.0)
            if h % 2 != g:
                t = pltpu.roll(t, NSA_HD, 1)
            q128[h] = t.astype(BF16)
            p = _masked_softmax(_dot_nt(q128[h], kvc[:, 0:LANES]), cmp_mask)
            p_cmp[h] = p.astype(BF16)
            imp_t = imp_t + _dot_nt(selmap_ref[...], p_cmp[h])
        imp_t = jnp.where(forced_t, FORCE_SCORE, imp_t)
        imp_t = jnp.where(valid_t, imp_t, -jnp.inf)
        sel_t = _topn_rows(imp_t, min(SEL_TOPN, n_slc)).astype(BF16)

        def sel_step(kc, carry):
            rows = pl.ds(pl.multiple_of(kc * ck, ck), ck)
            k = slc_ref[rows, 0:LANES].astype(BF16)
            v = slc_ref[rows, LANES:2 * LANES].astype(BF16)
            in_chunk = (chunk_row == kc * (ck // SEL_BLOCK) + chunk_blk).astype(BF16)
            picked = lax.dot_general(sel_t, in_chunk, (((0,), (0,)), ((), ())), preferred_element_type=F32)
            kmask = (picked > 0.5) & (kc * ck + chunk_pos <= qpos)
            new = []
            for h, (m, l, acc) in zip(heads, carry):
                s = jnp.where(kmask, _dot_nt(q128[h], k), -jnp.inf)
                m_new = jnp.maximum(m, jnp.max(s, -1, keepdims=True))
                a = jnp.exp(m - m_new)
                e = jnp.exp(s - m_new)
                new.append((m_new, a * l + jnp.sum(e, -1, keepdims=True),
                            a * acc + jnp.dot(e.astype(BF16), v, preferred_element_type=F32)))
            return tuple(new)

        init = tuple((jnp.full((qb, 1), -jnp.inf, F32), jnp.zeros((qb, 1), F32), jnp.zeros((qb, LANES), F32))
                     for _ in heads)
        sel_state = lax.fori_loop(0, n_chunks, sel_step, init)
        for h, (m, l, acc) in zip(heads, sel_state):
            o_cmp = jnp.dot(p_cmp[h], kvc[:, LANES:2 * LANES], preferred_element_type=F32)
            o_slc = acc * (1.0 / l)
            p = _masked_softmax(_dot_nt(q128[h], k_win), wmask)
            o_win = jnp.dot(p.astype(BF16), v_win, preferred_element_type=F32)
            o = (gates[:, 3 * h:3 * h + 1] * o_cmp + gates[:, 3 * h + 1:3 * h + 2] * o_slc
                 + gates[:, 3 * h + 2:3 * h + 3] * o_win)
            if h % 2 != g:
                o = pltpu.roll(o, NSA_HD, 1)
            keep = (lane // NSA_HD) == (h % 2)
            prev = out_tiles[h // 2]
            out_tiles[h // 2] = jnp.where(keep, o, 0.0 if prev is None else prev)
    for t, tile in enumerate(out_tiles):
        o_ref[:, t * LANES:(t + 1) * LANES] = tile


def _nsa_prompt(h, kvc, batch, seq):
    nqb = seq // Q_BLOCK
    n_cmp_pad = seq // CMP_STRIDE
    n_slc = seq // SEL_BLOCK
    assert seq % max(Q_BLOCK, min(NSA_SEL_CHUNK, seq)) == 0 and n_slc % 8 == 0 and n_slc >= SEL_LOCAL + 1
    win_span = min(NSA_WINDOW + Q_BLOCK, seq)
    sj = np.arange(n_slc)[:, None] * SEL_BLOCK
    ci = np.arange(n_cmp_pad)[None, :] * CMP_STRIDE
    selmap = np.clip(np.minimum(ci + CMP_BLOCK, sj + SEL_BLOCK) - np.maximum(ci, sj), 0, None) / CMP_STRIDE
    selmap[:, n_cmp_pad - 1:] = 0.0
    const = lambda a: pl.BlockSpec(a.shape, lambda b, i: (0,) * a.ndim)
    selmap = jnp.asarray(selmap, BF16)
    return pl.pallas_call(
        functools.partial(_nsa_prompt_kernel, seq=seq, win_span=win_span), grid=(batch, nqb),
        in_specs=[pl.BlockSpec((Q_BLOCK, 512), lambda b, i: (b * nqb + i, AB_Q // 512)),
                  pl.BlockSpec((Q_BLOCK, 256), lambda b, i: (b * nqb + i, AB_GATE // 256)),
                  pl.BlockSpec((seq, 256), lambda b, i: (b, (AB_KV + 256) // 256)),
                  pl.BlockSpec((seq, 256), lambda b, i: (b, (AB_KV + 512) // 256)),
                  pl.BlockSpec((1, n_cmp_pad, 256), lambda b, i: (b, 0, 0)),
                  const(selmap)],
        out_specs=pl.BlockSpec((Q_BLOCK, 512), lambda b, i: (b * nqb + i, 0)),
        out_shape=jax.ShapeDtypeStruct((batch * seq, 512), F32),
        compiler_params=_cparams(2), name="nsa_prompt")(h, h, h, h, kvc, selmap)


def _rope_tables(pos, half):
    inv = ROPE_THETA ** (-jnp.arange(half, dtype=F32) / half)
    ang = pos.astype(F32)[:, None] * inv[None, :]
    cos, sin = jnp.cos(ang), jnp.sin(ang)
    return jnp.concatenate([cos, cos], -1), jnp.concatenate([-sin, sin], -1)


def _retention_tables(chunk):
    log_g = jnp.log1p(-jnp.exp2(-5.0 - jnp.arange(RET_H, dtype=F32)))
    t = jnp.arange(chunk, dtype=F32)
    diff = t[:, None] - t[None, :]
    decay = jnp.where(diff >= 0, jnp.exp(log_g[:, None, None] * jnp.maximum(diff, 0.0)), 0.0)
    xi = jnp.exp(log_g[:, None] * (t + 1.0))
    zeta = jnp.exp(log_g[:, None] * (chunk - 1.0 - t))
    g_chunk = jnp.broadcast_to(jnp.exp(log_g * chunk)[:, None], (RET_H, chunk))
    cols = jnp.stack([xi, zeta, g_chunk], axis=-1)
    return decay, jnp.pad(cols, ((0, 0), (0, 0), (0, LANES - 3)))


def _head_norm_gate(o, gate):
    mu = jnp.mean(o, -1, keepdims=True)
    oc = o - mu
    var = jnp.mean(oc * oc, -1, keepdims=True)
    return oc * lax.rsqrt(var + LN_EPS) * (gate * jax.nn.sigmoid(gate))


def _rope_rows(x, cos, sin):
    return x * cos + pltpu.roll(x, x.shape[-1] // 2, 1) * sin


def _ret_prompt_kernel(q_ref, k_ref, v_ref, g_ref, cos_ref, sin_ref, decay_ref, cols_ref, o_ref, s_ref, *, chunk):
    n_chunks = q_ref.shape[0] // chunk
    s_ref[...] = jnp.zeros_like(s_ref)

    def body(c, carry):
        r = pl.ds(pl.multiple_of(c * chunk, chunk), chunk)
        cos, sin = cos_ref[r, :], sin_ref[r, :]
        for hh in range(RET_H):
            cols = slice(hh * LANES, (hh + 1) * LANES)
            xi, zeta, g_chunk = cols_ref[hh, :, 0:1], cols_ref[hh, :, 1:2], cols_ref[hh, 0:1, 2:3]
            state = s_ref[0, hh]
            q = _rope_rows(q_ref[r, cols], cos, sin)
            k = _rope_rows(k_ref[r, cols], cos, sin) * (RET_DK ** -0.5)
            v = v_ref[r, cols].astype(BF16)
            inner = _dot_nt(q.astype(BF16), k.astype(BF16)) * decay_ref[hh]
            o = (jnp.dot(inner.astype(BF16), v, preferred_element_type=F32)
                 + jnp.dot((q * xi).astype(BF16), state.astype(BF16), preferred_element_type=F32))
            o_ref[r, cols] = _head_norm_gate(o, g_ref[r, cols])
            kz = (k * zeta).astype(BF16)
            s_ref[0, hh] = g_chunk * state + lax.dot_general(kz, v, (((0,), (0,)), ((), ())),
                                                             preferred_element_type=F32)
        return carry

    lax.fori_loop(0, n_chunks, body, 0)


def _ret_prompt(h, batch, seq):
    chunk = math.gcd(seq, RET_CHUNK)
    assert chunk == RET_CHUNK and RET_DK == LANES and RET_DV == LANES
    cos, sin = _rope_tables(jnp.arange(seq), RET_DK // 2)
    decay, cols = _retention_tables(chunk)
    width = RET_H * LANES
    col = lambda base: pl.BlockSpec((seq, width), lambda b: (b, base // width))
    const = lambda a: pl.BlockSpec(a.shape, lambda b: (0,) * a.ndim)
    return pl.pallas_call(
        functools.partial(_ret_prompt_kernel, chunk=chunk), grid=(batch,),
        in_specs=[col(AB_QR), col(AB_KR), col(AB_VR), col(AB_GR), const(cos), const(sin), const(decay), const(cols)],
        out_specs=[pl.BlockSpec((seq, width), lambda b: (b, 0)),
                   pl.BlockSpec((1, RET_H, RET_DK, RET_DV), lambda b: (b, 0, 0, 0))],
        out_shape=[jax.ShapeDtypeStruct((batch * seq, RET_H * RET_DV), F32),
                   jax.ShapeDtypeStruct((batch, RET_H, RET_DK, RET_DV), F32)],
        compiler_params=_cparams(1), name="ret_prompt")(h, h, h, h, cos, sin, decay, cols)


def _dev_mixer_ab_prompt(x, w_in, w_out, w_cmp1, w_cmp2, cmp_pe):
    batch, seq, d = x.shape
    h = _mm(x.reshape(batch * seq, d), _prep_w_in_ab(w_in)[0])
    w1bd, w2bd, pe_l, _, _ = _prep_compress(w_cmp1, w_cmp2, cmp_pe)
    kvc = _compress_prompt(h, batch, seq, w1bd, w2bd, pe_l)
    o_nsa = _nsa_prompt(h, kvc, batch, seq)
    o_ret, s_new = _ret_prompt(h, batch, seq)
    mixed = jnp.concatenate([o_nsa, o_ret], axis=1)
    mix = _mm(mixed, w_out.astype(BF16))
    new_kv = h[:, AB_KV:AB_KV + 512].reshape(batch, seq, 4, NSA_G, NSA_HD)
    n_keep = min(NSA_WINDOW, seq)
    win = h[:, AB_KV + 512:AB_KV + 768].reshape(batch, seq, 2, NSA_G, NSA_HD)[:, seq - n_keep:]
    return mix.reshape(batch, seq, d), new_kv, win, s_new


def _page_copy(cache_ref, page, col0, width, dst_ref, sem):
    return pltpu.make_async_copy(cache_ref.at[page, :, pl.ds(col0, width)], dst_ref, sem)


def _topn_ids(imp, n_blocks, n_top, out_lane, lane0):
    lane = lax.broadcasted_iota(I32, imp.shape, 1)
    taken = lane >= n_blocks
    ids = jnp.zeros(out_lane.shape, I32)
    for t in range(n_top):
        m = jnp.max(jnp.where(taken, -jnp.inf, imp), axis=1, keepdims=True)
        idx = jnp.min(jnp.where(jnp.logical_not(taken) & (imp >= m), lane.astype(F32), float(imp.shape[1])),
                      axis=1, keepdims=True).astype(I32)
        ids = jnp.where(out_lane == lane0 + t, idx, ids)
        taken = taken | (lane == idx)
    return ids


def _nsa_sample_cmp_kernel(tbl_ref, cache_ref, q_ref, pe_ref, w1_ref, w2_ref, w1_lo_ref, w2_lo_ref, selmap_ref,
                           ocmp_ref, ids_ref,
                           buf, rows, sem, *, n_pages, past):
    b = pl.program_id(0)
    nb = pl.num_programs(0)
    slot = b % 2

    def page(bb, p, sl):
        return pltpu.make_async_copy(cache_ref.at[tbl_ref[bb * n_pages + p], pl.ds(0, 2 * LANES)], buf.at[sl, p],
                                     sem.at[sl])

    def fetch(bb, sl):
        def body(p, c):
            page(bb, p, sl).start()
            return c
        lax.fori_loop(0, n_pages, body, 0)

    @pl.when(b == 0)
    def _():
        fetch(b, slot)

    @pl.when(b + 1 < nb)
    def _():
        fetch(b + 1, 1 - slot)

    def wait(p, c):
        page(b, 0, slot).wait()
        return c
    lax.fori_loop(0, n_pages, wait, 0)

    def to_token_major(p, c):
        r = pl.ds(pl.multiple_of(p * PAGE, PAGE), PAGE)
        rows[0, r, :] = buf[slot, p, 0:LANES, :].T
        rows[1, r, :] = buf[slot, p, LANES:2 * LANES, :].T
        return c
    lax.fori_loop(0, n_pages, to_token_major, 0)

    n_chunks = past // CMP_STRIDE
    n_cmp = n_chunks - CMP_BLOCK // CMP_STRIDE + 1
    n_slc = past // SEL_BLOCK + 1
    k_cmp = _compress_rows(rows.at[0], 0, n_chunks, pe_ref, w1_ref, w2_ref, w1_lo_ref, w2_lo_ref)
    v_cmp = _compress_rows(rows.at[1], 1, n_chunks, pe_ref, w1_ref, w2_ref, w1_lo_ref, w2_lo_ref)
    q = q_ref[0] * (NSA_HD ** -0.5)
    cmp_idx = lax.broadcasted_iota(I32, (1, n_chunks), 1)
    cmp_mask = (cmp_idx < n_cmp) & (cmp_idx * CMP_STRIDE + (CMP_BLOCK - 1) <= past)
    p = _masked_softmax(_dot_split2(q, k_cmp, nt=True), cmp_mask)
    ocmp_ref[0] = _dot_split2(p, v_cmp)
    p_hi, p_lo = _split_bf16(p)
    imp_h = (jnp.dot(p_hi, selmap_ref[...], preferred_element_type=F32)
             + jnp.dot(p_lo, selmap_ref[...], preferred_element_type=F32))
    lane = lax.broadcasted_iota(I32, (1, imp_h.shape[1]), 1)
    cur = past // SEL_BLOCK
    forced = (lane == 0) | ((lane <= cur) & (lane > cur - SEL_LOCAL))
    out_lane = lax.broadcasted_iota(I32, (1, LANES), 1)
    ids = jnp.zeros((1, LANES), I32)
    for g in range(NSA_G):
        imp = jnp.sum(imp_h[g * NSA_HPG:(g + 1) * NSA_HPG], axis=0, keepdims=True)
        imp = jnp.where(forced, FORCE_SCORE, imp)
        imp = jnp.where(lane * SEL_BLOCK <= past, imp, -jnp.inf)
        ids = ids + _topn_ids(imp, n_slc, min(SEL_TOPN, n_slc), out_lane, g * SEL_TOPN)
    ids_ref[0] = ids


def _nsa_sample_cmp(table, cache, q128, w1bd, w2bd, pe_l, w1lo, w2lo, past):
    batch, n_pages = table.shape
    n_chunks = past // CMP_STRIDE
    n_slc = past // SEL_BLOCK + 1
    n_slc_pad = -(-n_slc // LANES) * LANES
    ci = np.arange(n_chunks)[:, None] * CMP_STRIDE
    sj = np.arange(n_slc_pad)[None, :] * SEL_BLOCK
    selmap = np.clip(np.minimum(ci + CMP_BLOCK, sj + SEL_BLOCK) - np.maximum(ci, sj), 0, None) / CMP_STRIDE
    selmap[n_chunks - 1:, :] = 0.0
    selmap[:, n_slc:] = 0.0
    selmap = jnp.asarray(selmap, BF16)
    const = lambda a: pl.BlockSpec(a.shape, lambda b, t: (0,) * a.ndim)
    return pl.pallas_call(
        functools.partial(_nsa_sample_cmp_kernel, n_pages=n_pages, past=past),
        grid_spec=pltpu.PrefetchScalarGridSpec(
            num_scalar_prefetch=1, grid=(batch,),
            in_specs=[pl.BlockSpec(memory_space=pl.ANY), pl.BlockSpec((1, NSA_H, LANES), lambda b, t: (b, 0, 0)),
                      const(pe_l), const(w1bd), const(w2bd), const(w1lo), const(w2lo), const(selmap)],
            out_specs=[pl.BlockSpec((1, NSA_H, LANES), lambda b, t: (b, 0, 0)),
                       pl.BlockSpec((1, 1, LANES), lambda b, t: (b, 0, 0))],
            scratch_shapes=[pltpu.VMEM((2, n_pages, 2 * LANES, PAGE), F32), pltpu.VMEM((2, past, LANES), F32),
                            pltpu.SemaphoreType.DMA((2,))]),
        out_shape=[jax.ShapeDtypeStruct((batch, NSA_H, LANES), F32), jax.ShapeDtypeStruct((batch, 1, LANES), I32)],
        compiler_params=_cparams(1), name="nsa_sample_cmp")(
            table.reshape(-1), cache, q128, pe_l, w1bd, w2bd, w1lo, w2lo, selmap)


def _nsa_sample_attend_kernel(tbl_ref, ids_ref, cache_ref, q_ref, ocmp_ref, gate_ref, new_ref, win_ref,
                              o_ref, wout_ref, sbuf, sem, *, n_pages, past):
    b = pl.program_id(0)
    nb = pl.num_programs(0)
    slot = b % 2
    n_sel = NSA_G * SEL_TOPN
    last_blk = past // SEL_BLOCK
    half = SEL_BLOCK
    per_page = PAGE // SEL_BLOCK

    def block_copy(bb, i, sl):
        blk = jnp.minimum(ids_ref[bb * n_sel + i], last_blk - 1)
        page = tbl_ref[bb * n_pages + blk // per_page]
        return pltpu.make_async_copy(cache_ref.at[page, pl.ds(2 * LANES, 2 * LANES)], sbuf.at[sl, i], sem.at[sl])

    def fetch(bb, sl):
        for i in range(n_sel):
            block_copy(bb, i, sl).start()

    @pl.when(b == 0)
    def _():
        fetch(b, slot)

    @pl.when(b + 1 < nb)
    def _():
        fetch(b + 1, 1 - slot)

    n_win = win_ref.shape[2]
    for i in range(n_sel):
        block_copy(b, i, slot).wait()

    q = (q_ref[0] * (NSA_HD ** -0.5))
    new = new_ref[0]
    gates = jax.nn.sigmoid(gate_ref[0])
    row = lax.broadcasted_iota(I32, (NSA_H, 1), 0)

    def attend(s, mask, v, s_new, v_new, new_ok):
        s = jnp.where(mask, s, -jnp.inf)
        s_new = jnp.where(new_ok, s_new, -jnp.inf)
        m = jnp.maximum(jnp.max(s, -1, keepdims=True), s_new)
        m = jnp.where(m == -jnp.inf, 0.0, m)
        e, e_new = jnp.exp(s - m), jnp.exp(s_new - m)
        den = jnp.maximum(jnp.sum(e, -1, keepdims=True) + e_new, jnp.finfo(F32).tiny)
        return (_dot_split2(e, v) + e_new * v_new) * (1.0 / den)

    o_slc = jnp.zeros((NSA_H, LANES), F32)
    key_lane = lax.broadcasted_iota(I32, (1, SEL_TOPN * PAGE), 1)
    s_new = jnp.sum(q * new[:, 0:LANES], axis=-1, keepdims=True)
    for g in range(NSA_G):
        kv = jnp.concatenate([sbuf[slot, g * SEL_TOPN + i].T for i in range(SEL_TOPN)], axis=0)
        ok = jnp.zeros((1, SEL_TOPN * PAGE), jnp.bool_)
        has_new = False
        for i in range(SEL_TOPN):
            blk = ids_ref[b * n_sel + g * SEL_TOPN + i]
            in_block = (key_lane // PAGE == i) & ((key_lane % PAGE) // half == blk % per_page)
            ok = ok | (in_block & (blk != last_blk))
            has_new = has_new | (blk == last_blk)
        o = attend(_dot_split2(q, kv[:, 0:LANES], nt=True), ok, kv[:, LANES:2 * LANES],
                   s_new, new[:, LANES:2 * LANES], has_new)
        o_slc = jnp.where(row // NSA_HPG == g, o, o_slc)

    win = win_ref[0].T
    wpos = past - n_win + lax.broadcasted_iota(I32, (1, n_win), 1)
    wmask = (wpos >= 0) & (wpos > past - NSA_WINDOW)
    sw_new = jnp.sum(q * new[:, 2 * LANES:3 * LANES], axis=-1, keepdims=True)
    o_win = attend(_dot_split2(q, win[:, 0:LANES], nt=True), wmask, win[:, LANES:2 * LANES],
                   sw_new, new[:, 3 * LANES:4 * LANES], True)

    o = gates[:, 0:1] * ocmp_ref[0] + gates[:, 1:2] * o_slc + gates[:, 2:3] * o_win
    o = jnp.where(row // NSA_HPG == 0, o, pltpu.roll(o, NSA_HD, 1))
    o_ref[0] = o[:, 0:NSA_HD]
    win_row = lax.broadcasted_iota(I32, (n_win, 1), 0)
    wout_ref[0] = jnp.where(win_row == n_win - 1, new[:, 2 * LANES:4 * LANES], pltpu.roll(win, n_win - 1, 0))


def _nsa_sample_attend(table, ids, cache, q128, o_cmp, gates, new, win, past):
    batch, n_pages = table.shape
    n_win = win.shape[2]
    assert n_win == NSA_WINDOW and past >= NSA_WINDOW
    per_b = lambda shape: pl.BlockSpec((1,) + shape, lambda b, t, i: (b, 0, 0))
    return pl.pallas_call(
        functools.partial(_nsa_sample_attend_kernel, n_pages=n_pages, past=past),
        grid_spec=pltpu.PrefetchScalarGridSpec(
            num_scalar_prefetch=2, grid=(batch,),
            in_specs=[pl.BlockSpec(memory_space=pl.ANY), per_b((NSA_H, LANES)), per_b((NSA_H, LANES)),
                      per_b((NSA_H, LANES)), per_b((1, 512)), per_b((256, n_win))],
            out_specs=[per_b((NSA_H, NSA_HD)), per_b((n_win, 256))],
            scratch_shapes=[pltpu.VMEM((2, NSA_G * SEL_TOPN, 2 * LANES, PAGE), F32), pltpu.SemaphoreType.DMA((2,))]),
        out_shape=[jax.ShapeDtypeStruct((batch, NSA_H, NSA_HD), F32), jax.ShapeDtypeStruct((batch, n_win, 256), F32)],
        compiler_params=_cparams(1), name="nsa_sample_attend")(
            table.reshape(-1), ids, cache, q128, o_cmp, gates, new, win)


def _ret_sample_kernel(q_ref, k_ref, v_ref, g_ref, cos_ref, sin_ref, gam_ref, s0_ref, o_ref, s_ref):
    cos, sin = cos_ref[...], sin_ref[...]
    q = _rope_rows(q_ref[0], cos, sin)
    k = _rope_rows(k_ref[0], cos, sin) * (RET_DK ** -0.5)
    v = v_ref[0]
    gam = gam_ref[...]
    inner = jnp.sum(q * k, axis=-1, keepdims=True)
    eye = lax.broadcasted_iota(I32, (RET_DK, RET_DK), 0) == lax.broadcasted_iota(I32, (RET_DK, RET_DK), 1)
    qx = q * gam
    rows = []
    for h in range(RET_H):
        s0 = s0_ref[0, h]
        rows.append(_dot_split(qx, *_split_bf16(s0))[h:h + 1])
        k_col = jnp.sum(jnp.where(eye, k[h:h + 1], 0.0), axis=1, keepdims=True)
        s_ref[0, h] = gam[h:h + 1, 0:1] * s0 + k_col * v[h:h + 1]
    o = inner * v + jnp.concatenate(rows, axis=0)
    o_ref[0] = _head_norm_gate(o, g_ref[0])


def _ret_sample(q, k, v, g, s0, past):
    batch = q.shape[0]
    cos, sin = _rope_tables(jnp.full((1,), past), RET_DK // 2)
    gam = jnp.exp(jnp.log1p(-jnp.exp2(-5.0 - jnp.arange(RET_H, dtype=F32))))[:, None] * jnp.ones((1, LANES), F32)
    row = pl.BlockSpec((1, RET_H, LANES), lambda b: (b, 0, 0))
    const = lambda a: pl.BlockSpec(a.shape, lambda b: (0,) * a.ndim)
    state = pl.BlockSpec((1, RET_H, RET_DK, RET_DV), lambda b: (b, 0, 0, 0))
    return pl.pallas_call(
        _ret_sample_kernel, grid=(batch,),
        in_specs=[row, row, row, row, const(cos), const(sin), const(gam), state],
        out_specs=[row, state],
        out_shape=[jax.ShapeDtypeStruct((batch, RET_H, RET_DV), F32), jax.ShapeDtypeStruct(s0.shape, F32)],
        compiler_params=_cparams(1), name="ret_sample")(q, k, v, g, cos, sin, gam, s0)


def _group_lanes(q):
    g = (jnp.arange(NSA_H) // NSA_HPG)[None, :, None]
    z = jnp.zeros_like(q)
    return jnp.concatenate([jnp.where(g == 0, q, z), jnp.where(g == 1, q, z)], axis=-1)


def _mixer_ab_sample(hs, cache, win, s0, table, w1bd, w2bd, pe_l, w1lo, w2lo):
    batch = hs.shape[0]
    past = table.shape[1] * PAGE
    cache = jnp.transpose(cache, (0, 2, 3, 4, 1)).reshape(cache.shape[0], 4 * NSA_G * NSA_HD, PAGE)
    win_t = jnp.transpose(win, (0, 2, 3, 4, 1)).reshape(batch, 2 * NSA_G * NSA_HD, win.shape[1])
    q128 = _group_lanes(hs[:, AB_Q:AB_Q + 512].reshape(batch, NSA_H, NSA_HD))
    o_cmp, ids = _nsa_sample_cmp(table, cache, q128, w1bd, w2bd, pe_l, w1lo, w2lo, past)
    gates = jnp.pad(hs[:, AB_GATE:AB_GATE + 24].reshape(batch, NSA_H, 3), ((0, 0), (0, 0), (0, LANES - 3)))
    new = hs[:, AB_KV + 256:AB_KV + 768].reshape(batch, 1, 512)
    o_nsa, win_new = _nsa_sample_attend(table, ids[:, 0, :NSA_G * SEL_TOPN].reshape(-1), cache, q128, o_cmp, gates,
                                        new, win_t, past)
    seg = lambda c: hs[:, c:c + 512].reshape(batch, RET_H, RET_DK)
    o_ret, s_new = _ret_sample(seg(AB_QR), seg(AB_KR), seg(AB_VR), seg(AB_GR), s0, past)
    mixed = jnp.concatenate([o_nsa.reshape(batch, 512), o_ret.reshape(batch, 512)], axis=1)
    return mixed, win_new.reshape(win.shape), s_new


def _dev_mixer_ab_sample(x, cache, win, s0, table, w_in, w_out, w_cmp1, w_cmp2, cmp_pe):
    batch = x.shape[0]
    xs = jnp.pad(x.reshape(batch, -1), ((0, TOK_TILE - batch), (0, 0)))
    hs = _mm(xs, *_prep_w_in_ab(w_in), precise_from=0)[:batch]
    mixed, win_new, s_new = _mixer_ab_sample(hs, cache, win, s0, table, *_prep_compress(w_cmp1, w_cmp2, cmp_pe))
    mix = _mm(jnp.pad(mixed, ((0, TOK_TILE - batch), (0, 0))), *_split_bf16(w_out), precise_from=0)[:batch]
    new_kv = hs[:, AB_KV:AB_KV + 512].reshape(batch, 1, 4, NSA_G, NSA_HD)
    return mix.reshape(batch, 1, -1), new_kv, win_new, s_new


MLA_ROW_W = 384
MLA_QK = LANES
MLA_CW = 1024
MLA_KR, MLA_KR_ROT = 640, 768


def _rot_half_cols(w):
    half = w.shape[-1] // 2
    return jnp.concatenate([-w[..., half:], w[..., :half]], axis=-1)


def _prep_mla(w_in, w_qb, w_kb, w_vb):
    d = w_in.shape[0]
    kr = w_in[:, MLA_Q_LORA + MLA_KV_LORA:]
    z = lambda n: jnp.zeros((d, n), w_in.dtype)
    wc = jnp.concatenate([w_in[:, :MLA_Q_LORA + MLA_KV_LORA], kr, z(LANES - MLA_ROPE), _rot_half_cols(kr),
                          z(MLA_CW - MLA_KR_ROT - MLA_ROPE)], axis=1)
    zq = jnp.zeros((MLA_Q_LORA, MLA_H, MLA_QK - MLA_NOPE - MLA_ROPE), w_qb.dtype)
    wq = jnp.concatenate([w_qb, zq], axis=-1).reshape(MLA_Q_LORA, MLA_H * MLA_QK)
    wq_rot = jnp.concatenate([jnp.zeros_like(w_qb[..., :MLA_NOPE]), _rot_half_cols(w_qb[..., MLA_NOPE:]), zq],
                             axis=-1).reshape(MLA_Q_LORA, MLA_H * MLA_QK)
    eye = jnp.eye(MLA_ROPE, dtype=w_kb.dtype)
    wk = jnp.zeros((MLA_ROW_W, MLA_H, MLA_QK), w_kb.dtype)
    wk = wk.at[:MLA_KV_LORA, :, :MLA_NOPE].set(w_kb)
    wk = wk.at[MLA_KV_LORA:MLA_KV_LORA + MLA_ROPE, :, MLA_NOPE:MLA_NOPE + MLA_ROPE].set(
        jnp.broadcast_to(eye[:, None, :], (MLA_ROPE, MLA_H, MLA_ROPE)))
    wv = jnp.zeros((MLA_ROW_W, MLA_H, MLA_V), w_vb.dtype).at[:MLA_KV_LORA].set(w_vb)
    wkv = jnp.concatenate([wk.reshape(MLA_ROW_W, -1), wv.reshape(MLA_ROW_W, -1)], axis=1)
    wabs = jnp.zeros((MLA_H, MLA_QK, MLA_ROW_W), w_kb.dtype)
    wabs = wabs.at[:, :MLA_NOPE, :MLA_KV_LORA].set(jnp.transpose(w_kb, (1, 2, 0)))
    wabs = wabs.at[:, MLA_NOPE:MLA_NOPE + MLA_ROPE, MLA_KV_LORA:MLA_KV_LORA + MLA_ROPE].set(
        jnp.broadcast_to(eye[None], (MLA_H, MLA_ROPE, MLA_ROPE)))
    wvt = jnp.transpose(w_vb, (1, 0, 2))
    return (wc.astype(BF16), wq.astype(BF16), wq_rot.astype(BF16), wkv.astype(BF16), wabs.astype(BF16),
            wvt.astype(BF16))


def _mla_rope_tables(pos):
    half = MLA_ROPE // 2
    inv = ROPE_THETA ** (-jnp.arange(half, dtype=F32) / half)
    ang = pos.astype(F32)[:, None] * inv[None, :]
    cos, sin = jnp.cos(ang), jnp.sin(ang)
    n = pos.shape[0]
    cos2, sin2 = jnp.concatenate([cos, cos], -1), jnp.concatenate([sin, sin], -1)
    z = lambda w: jnp.zeros((n, w), F32)
    cos_k = jnp.concatenate([cos2, z(LANES - MLA_ROPE)], -1)
    sin_k = jnp.concatenate([sin2, z(LANES - MLA_ROPE)], -1)
    cos_q = jnp.concatenate([jnp.ones((n, MLA_NOPE), F32), cos2, z(MLA_QK - MLA_NOPE - MLA_ROPE)], -1)
    sin_q = jnp.concatenate([z(MLA_NOPE), sin2, z(MLA_QK - MLA_NOPE - MLA_ROPE)], -1)
    return cos_k, sin_k, cos_q, sin_q


def _rms_rows(x, g):
    return x * lax.rsqrt(jnp.mean(x * x, -1, keepdims=True) + RMS_EPS) * g


def _mla_in_kernel(x_ref, w_ref, gq_ref, gkv_ref, cos_ref, sin_ref, cq_ref, rows_ref):
    xb = x_ref[...].astype(BF16)
    h = jnp.concatenate([jnp.dot(xb, w_ref[:, c:c + 2 * MXU_N], preferred_element_type=F32)
                         for c in range(0, MLA_CW, 2 * MXU_N)], axis=1)
    cq_ref[...] = _rms_rows(h[:, :MLA_Q_LORA], gq_ref[...]).astype(cq_ref.dtype)
    rows_ref[:, :MLA_KV_LORA] = _rms_rows(h[:, MLA_Q_LORA:MLA_Q_LORA + MLA_KV_LORA], gkv_ref[...])
    rows_ref[:, MLA_KV_LORA:] = (h[:, MLA_KR:MLA_KR + LANES] * cos_ref[...]
                                 + h[:, MLA_KR_ROT:MLA_KR_ROT + LANES] * sin_ref[...])


def _pos_block(n_prompt_tiles, tiles_per_seq):
    return lambda i: (jnp.where(i < n_prompt_tiles, i % tiles_per_seq, tiles_per_seq + i - n_prompt_tiles), 0)


def _mla_in(x, wc, gq, gkv, cos_k, sin_k, n_prompt_tiles, tiles_per_seq, tile=TOK_TILE):
    n, d = x.shape
    pos = pl.BlockSpec((tile, LANES), _pos_block(n_prompt_tiles, tiles_per_seq))
    vec = lambda w: pl.BlockSpec((1, w), lambda i: (0, 0))
    return pl.pallas_call(
        _mla_in_kernel, grid=(n // tile,),
        in_specs=[pl.BlockSpec((tile, d), lambda i: (i, 0)), pl.BlockSpec((d, MLA_CW), lambda i: (0, 0)),
                  vec(MLA_Q_LORA), vec(MLA_KV_LORA), pos, pos],
        out_specs=[pl.BlockSpec((tile, MLA_Q_LORA), lambda i: (i, 0)), pl.BlockSpec((tile, MLA_ROW_W), lambda i: (i, 0))],
        out_shape=[jax.ShapeDtypeStruct((n, MLA_Q_LORA), BF16), jax.ShapeDtypeStruct((n, MLA_ROW_W), F32)],
        compiler_params=_cparams(1), name="mla_in")(
            x, wc, gq.reshape(1, -1), gkv.reshape(1, -1), cos_k, sin_k)


def _mla_q_kernel(c_ref, w_ref, wrot_ref, cos_ref, sin_ref, q_ref):
    c = c_ref[...]
    cos, sin = cos_ref[...], sin_ref[...]
    for h in range(MLA_H):
        cols = slice(h * MLA_QK, (h + 1) * MLA_QK)
        q = jnp.dot(c, w_ref[:, cols], preferred_element_type=F32)
        qr = jnp.dot(c, wrot_ref[:, cols], preferred_element_type=F32)
        q_ref[:, cols] = (q * cos + qr * sin).astype(q_ref.dtype)


def _mla_q(cq, wq, wq_rot, cos_q, sin_q, n_prompt_tiles, tiles_per_seq, tile=TOK_TILE):
    n, k = cq.shape
    m = wq.shape[1]
    pos = pl.BlockSpec((tile, LANES), _pos_block(n_prompt_tiles, tiles_per_seq))
    w = pl.BlockSpec((k, m), lambda i: (0, 0))
    return pl.pallas_call(
        _mla_q_kernel, grid=(n // tile,),
        in_specs=[pl.BlockSpec((tile, k), lambda i: (i, 0)), w, w, pos, pos],
        out_specs=pl.BlockSpec((tile, m), lambda i: (i, 0)),
        out_shape=jax.ShapeDtypeStruct((n, m), BF16),
        compiler_params=_cparams(1), name="mla_q")(cq, wq, wq_rot, cos_q, sin_q)


MLA_ATT_TILE = 512
MLA_ATT_CHUNK = 256


def _mla_prompt_kernel(q_ref, k_ref, v_ref, o_ref):
    ck = MLA_ATT_CHUNK
    n_sub = q_ref.shape[0] // ck
    qi = pl.program_id(2)
    c2 = (MLA_NOPE + MLA_ROPE) ** -0.5 * math.log2(math.e)
    lane = lax.broadcasted_iota(I32, (ck, LANES), 1)
    diag = lax.broadcasted_iota(I32, (ck, ck), 1) <= lax.broadcasted_iota(I32, (ck, ck), 0)
    problems = [(hh, r) for r in range(n_sub) for hh in range(2)]
    q = {(hh, r): q_ref[r * ck:(r + 1) * ck, hh * MLA_QK:(hh + 1) * MLA_QK] for hh, r in problems}

    def step(prob, kc, carry, masked):
        hh, _ = prob
        m, l, acc = carry
        rows = pl.ds(pl.multiple_of(kc * ck, ck), ck)
        s = _dot_nt(q[prob], k_ref[rows, hh * MLA_QK:(hh + 1) * MLA_QK])
        if masked:
            s = jnp.where(diag, s, -jnp.inf)
        m_new = jnp.maximum(m, jnp.max(s, -1, keepdims=True))
        a = jnp.exp2((m - m_new) * c2)
        e = jnp.exp2((s - m_new) * c2)
        l = a * l + jnp.sum(e, -1, keepdims=True)
        acc = a * acc + jnp.dot(e.astype(BF16), v_ref[rows, :], preferred_element_type=F32)
        return m_new, l, acc

    init = tuple((jnp.full((ck, 1), -jnp.inf, F32), jnp.zeros((ck, 1), F32), jnp.zeros((ck, LANES), F32))
                 for _ in problems)

    def full_chunks(kc, carries):
        return tuple(step(p, kc, c, False) for p, c in zip(problems, carries))

    carries = list(lax.fori_loop(0, n_sub * qi, full_chunks, init))
    for idx, prob in enumerate(problems):
        r = prob[1]
        for c in range(r + 1):
            carries[idx] = step(prob, n_sub * qi + c, carries[idx], c == r)
    for r in range(n_sub):
        outs = []
        for hh in range(2):
            m, l, acc = carries[problems.index((hh, r))]
            outs.append(acc * (1.0 / l))
        o_ref[r * ck:(r + 1) * ck, :] = jnp.where(lane < MLA_V, outs[0], outs[1]).astype(o_ref.dtype)


def _mla_prompt(q, kv, batch, seq):
    tq = min(MLA_ATT_TILE, seq)
    assert tq % MLA_ATT_CHUNK == 0
    nq = seq // tq
    k_cols = MLA_H * MLA_QK // (2 * MLA_QK)
    return pl.pallas_call(
        _mla_prompt_kernel, grid=(batch, MLA_H // 2, nq),
        in_specs=[pl.BlockSpec((tq, 2 * MLA_QK), lambda b, j, i: (b * nq + i, j)),
                  pl.BlockSpec((seq, 2 * MLA_QK), lambda b, j, i: (b, j)),
                  pl.BlockSpec((seq, 2 * MLA_V), lambda b, j, i: (b, 2 * k_cols + j))],
        out_specs=pl.BlockSpec((tq, 2 * MLA_V), lambda b, j, i: (b * nq + i, j)),
        out_shape=jax.ShapeDtypeStruct((batch * seq, MLA_H * MLA_V), BF16),
        compiler_params=_cparams(3), name="mla_prompt")(q, kv, kv)


def _mla_absorb_kernel(q_ref, w_ref, o_ref):
    o_ref[0] = jnp.dot(q_ref[...], w_ref[0], preferred_element_type=F32)


def _mla_absorb(qs, wabs):
    batch = qs.shape[0]
    return pl.pallas_call(
        _mla_absorb_kernel, grid=(MLA_H,),
        in_specs=[pl.BlockSpec((batch, MLA_QK), lambda h: (0, h)), pl.BlockSpec((1, MLA_QK, MLA_ROW_W), lambda h: (h, 0, 0))],
        out_specs=pl.BlockSpec((1, batch, MLA_ROW_W), lambda h: (h, 0, 0)),
        out_shape=jax.ShapeDtypeStruct((MLA_H, batch, MLA_ROW_W), F32),
        compiler_params=_cparams(1), name="mla_absorb")(qs, wabs)


def _mla_sample_kernel(tbl_ref, cache_ref, q_ref, new_ref, o_ref, buf, s_scr, e_scr, sem, *, n_pages):
    b = pl.program_id(0)
    nb = pl.num_programs(0)
    slot = b % 2
    width = MLA_KV_LORA + MLA_ROPE

    def page(bb, p, sl):
        return pltpu.make_async_copy(cache_ref.at[tbl_ref[bb * n_pages + p]], buf.at[sl, p], sem.at[sl])

    def fetch(bb, sl):
        def body(p, c):
            page(bb, p, sl).start()
            return c
        lax.fori_loop(0, n_pages, body, 0)

    @pl.when(b == 0)
    def _():
        fetch(b, slot)

    @pl.when(b + 1 < nb)
    def _():
        fetch(b + 1, 1 - slot)

    def wait(p, c):
        page(b, 0, slot).wait()
        return c
    lax.fori_loop(0, n_pages, wait, 0)

    scale = (MLA_NOPE + MLA_ROPE) ** -0.5
    q = q_ref[0]
    qb = q[:, :width].astype(BF16)

    def score(p, c):
        s_scr[p] = jnp.dot(qb, buf[slot, p].astype(BF16), preferred_element_type=F32)
        return c
    lax.fori_loop(0, n_pages, score, 0, unroll=4)

    s = s_scr[...] * scale
    new = new_ref[0]
    s_new = jnp.sum(q * new, axis=-1, keepdims=True) * scale
    m = jnp.maximum(jnp.max(jnp.max(s, axis=0), -1, keepdims=True), s_new)
    e, e_new = jnp.exp(s - m), jnp.exp(s_new - m)
    den = jnp.sum(jnp.sum(e, axis=0), -1, keepdims=True) + e_new
    e_scr[...] = e.astype(BF16)

    def weighted(p, acc):
        return acc + _dot_nt(e_scr[p], buf[slot, p, 0:MLA_KV_LORA, :].astype(BF16))
    o = lax.fori_loop(0, n_pages, weighted, jnp.zeros((MLA_H, MLA_KV_LORA), F32), unroll=4)
    o_ref[0] = (o + e_new * new[:, :MLA_KV_LORA]) * (1.0 / den)


def _mla_sample(table, cache, qabs, new_rows):
    batch, n_pages = table.shape
    width = cache.shape[1]
    return pl.pallas_call(
        functools.partial(_mla_sample_kernel, n_pages=n_pages),
        grid_spec=pltpu.PrefetchScalarGridSpec(
            num_scalar_prefetch=1, grid=(batch,),
            in_specs=[pl.BlockSpec(memory_space=pl.ANY), pl.BlockSpec((1, MLA_H, MLA_ROW_W), lambda b, t: (b, 0, 0)),
                      pl.BlockSpec((1, 1, MLA_ROW_W), lambda b, t: (b, 0, 0))],
            out_specs=pl.BlockSpec((1, MLA_H, MLA_KV_LORA), lambda b, t: (b, 0, 0)),
            scratch_shapes=[pltpu.VMEM((2, n_pages, width, PAGE), F32), pltpu.VMEM((n_pages, MLA_H, PAGE), F32),
                            pltpu.VMEM((n_pages, MLA_H, PAGE), BF16), pltpu.SemaphoreType.DMA((2,))]),
        out_shape=jax.ShapeDtypeStruct((batch, MLA_H, MLA_KV_LORA), F32),
        compiler_params=_cparams(1), name="mla_sample")(table.reshape(-1), cache, qabs, new_rows)


def _mla_vup_kernel(o_ref, w_ref, y_ref):
    y_ref[0] = jnp.dot(o_ref[0].astype(BF16), w_ref[0], preferred_element_type=F32)


def _mla_vup(o_lat, wvt):
    _, batch, lat = o_lat.shape
    return pl.pallas_call(
        _mla_vup_kernel, grid=(MLA_H,),
        in_specs=[pl.BlockSpec((1, batch, lat), lambda h: (h, 0, 0)), pl.BlockSpec((1, lat, MLA_V), lambda h: (h, 0, 0))],
        out_specs=pl.BlockSpec((1, batch, MLA_V), lambda h: (h, 0, 0)),
        out_shape=jax.ShapeDtypeStruct((MLA_H, batch, MLA_V), F32),
        compiler_params=_cparams(1), name="mla_vup")(o_lat, wvt)


def _mixer_c(x, n_prompt, batch_p, seq, batch_s, cache, table, w_in, gq, gkv, w_qb, w_kb, w_vb):
    n = x.shape[0]
    past = table.shape[1] * PAGE
    wc, wq, wq_rot, wkv, wabs, wvt = _prep_mla(w_in, w_qb, w_kb, w_vb)
    pos = jnp.concatenate([jnp.arange(seq), jnp.full((n - n_prompt,), past)])
    cos_k, sin_k, cos_q, sin_q = _mla_rope_tables(pos)
    n_pt, tps = n_prompt // TOK_TILE, seq // TOK_TILE
    cq, rows = _mla_in(x, wc, gq, gkv, cos_k, sin_k, n_pt, tps)
    q = _mla_q(cq, wq, wq_rot, cos_q, sin_q, n_pt, tps)
    kv = _mm(rows, wkv, out_dtype=BF16)
    attn_p = _mla_prompt(q, kv, batch_p, seq)
    qabs = _mla_absorb(q[n_prompt:n_prompt + batch_s], wabs)
    new_rows = rows[n_prompt:n_prompt + batch_s].reshape(batch_s, 1, MLA_ROW_W)
    o_lat = _mla_sample(table, jnp.transpose(cache, (0, 2, 1)), jnp.transpose(qabs, (1, 0, 2)), new_rows)
    o_s = _mla_vup(jnp.transpose(o_lat, (1, 0, 2)), wvt)
    o_s = jnp.transpose(o_s, (1, 0, 2)).reshape(batch_s, MLA_H * MLA_V).astype(BF16)
    attn = jnp.concatenate([attn_p, o_s, jnp.zeros((n - n_prompt - batch_s, MLA_H * MLA_V), BF16)], axis=0)
    return attn, rows


def _dev_mixer_c(xp, xs, cache, table, w_in, gq, gkv, w_qb, w_kb, w_vb, w_out):
    bp, seq, d = xp.shape
    bs = xs.shape[0]
    n_prompt = bp * seq
    x = jnp.concatenate([xp.reshape(n_prompt, d), xs.reshape(bs, d), jnp.zeros((TOK_TILE - bs, d), F32)], axis=0)
    attn, rows = _mixer_c(x, n_prompt, bp, seq, bs, cache, table, w_in, gq, gkv, w_qb, w_kb, w_vb)
    mix = _mm(attn, w_out.astype(BF16))
    width = MLA_KV_LORA + MLA_ROPE
    return ((mix[:n_prompt].reshape(bp, seq, d), rows[:n_prompt, :width].reshape(bp, seq, width)),
            (mix[n_prompt:n_prompt + bs].reshape(bs, 1, d), rows[n_prompt:n_prompt + bs, :width].reshape(bs, 1, width)))


def kernel(x_prompt, x_sample, cache_nsa_kv, state_nsa_win, state_ret, cache_mla, page_table, w_in_ab, w_out_ab, w_cmp1, w_cmp2, cmp_pe, w_in_mla, mla_q_norm, mla_kv_norm, w_q_up, w_k_up, w_v_up, w_out_mla, ln_mix_g, ln_mix_b, ln_ffn_g, ln_ffn_b, w_router, router_bias, w_exp_gate, w_exp_up, w_exp_down):
    bp, seq, d = x_prompt.shape
    bs = x_sample.shape[0]
    assert x_sample.shape[1] == 1 and bs <= TOK_TILE and seq % TOK_TILE == 0 and ln_mix_g.shape[0] == DEPTH
    n_prompt = bp * seq
    pad_rows = TOK_TILE - bs
    x = jnp.concatenate([x_prompt.reshape(n_prompt, d), x_sample.reshape(bs, d), jnp.zeros((pad_rows, d), F32)], axis=0)
    sample = slice(n_prompt, n_prompt + bs)
    sample_tile = n_prompt // TOK_TILE
    mla_w = MLA_KV_LORA + MLA_ROPE
    kv_p, kv_s, win_p, win_s, ret_p, ret_s, mla_p, mla_s = [], [], [], [], [], [], [], []
    for layer in range(DEPTH):
        i = layer // 2
        if layer % 2 == 0:
            h = _mm(x, *_prep_w_in_ab(w_in_ab[i]), precise_from=sample_tile)
            w1bd, w2bd, pe_l, w1lo, w2lo = _prep_compress(w_cmp1[i], w_cmp2[i], cmp_pe[i])
            kvc = _compress_prompt(h, bp, seq, w1bd, w2bd, pe_l)
            o_nsa = _nsa_prompt(h, kvc, bp, seq)
            o_ret, s_p = _ret_prompt(h, bp, seq)
            hs = h[sample]
            mixed_s, w_s, s_s = _mixer_ab_sample(hs, cache_nsa_kv[i], state_nsa_win[i], state_ret[i], page_table,
                                                 w1bd, w2bd, pe_l, w1lo, w2lo)
            mixed = jnp.concatenate([jnp.concatenate([o_nsa, o_ret], axis=1), mixed_s,
                                     jnp.zeros((pad_rows, mixed_s.shape[1]), F32)], axis=0)
            w_out, w_out_lo = _split_bf16(w_out_ab[i])
            out_precise = sample_tile
            kv_p.append(h[:n_prompt, AB_KV:AB_KV + 512].reshape(bp, seq, 4, NSA_G, NSA_HD))
            kv_s.append(hs[:, AB_KV:AB_KV + 512].reshape(bs, 1, 4, NSA_G, NSA_HD))
            n_keep = min(NSA_WINDOW, seq)
            win_p.append(h[:n_prompt, AB_KV + 512:AB_KV + 768].reshape(bp, seq, 2, NSA_G, NSA_HD)[:, seq - n_keep:])
            win_s.append(w_s)
            ret_p.append(s_p)
            ret_s.append(s_s)
        else:
            mixed, rows = _mixer_c(x, n_prompt, bp, seq, bs, cache_mla[i], page_table, w_in_mla[i], mla_q_norm[i],
                                   mla_kv_norm[i], w_q_up[i], w_k_up[i], w_v_up[i])
            w_out, w_out_lo, out_precise = w_out_mla[i].astype(BF16), None, None
            mla_p.append(rows[:n_prompt, :mla_w].reshape(bp, seq, mla_w))
            mla_s.append(rows[sample, :mla_w].reshape(bs, 1, mla_w))
        x = _proj_ln(mixed, w_out, x, ln_mix_g[layer], ln_mix_b[layer], w_lo=w_out_lo, precise_from=out_precise)
        x = _moe_ln(x, w_router, router_bias, w_exp_gate[layer].astype(BF16), w_exp_up[layer].astype(BF16),
                    w_exp_down[layer].astype(BF16), ln_ffn_g[layer], ln_ffn_b[layer])
    return (x[:n_prompt].reshape(bp, seq, d), x[sample].reshape(bs, 1, d), jnp.stack(kv_p), jnp.stack(kv_s),
            jnp.stack(win_p), jnp.stack(win_s), jnp.stack(ret_p), jnp.stack(ret_s), jnp.stack(mla_p), jnp.stack(mla_s))
```

```python
import functools
import math

import numpy as np
import jax
import jax.numpy as jnp
from jax import lax
from jax.experimental import pallas as pl
from jax.experimental.pallas import tpu as pltpu

F32 = jnp.float32
BF16 = jnp.bfloat16
I32 = jnp.int32

PAGE = 128
NSA_H, NSA_G, NSA_HD = 8, 2, 64
NSA_HPG = NSA_H // NSA_G
CMP_BLOCK, CMP_STRIDE, CMP_HIDDEN = 32, 16, 64
SEL_BLOCK, SEL_TOPN, SEL_LOCAL = 64, 8, 2
NSA_WINDOW = 512
FORCE_SCORE = 1.0e6
RET_H, RET_DK, RET_DV, RET_CHUNK = 4, 128, 128, 128
MLA_H, MLA_Q_LORA, MLA_KV_LORA, MLA_NOPE, MLA_ROPE, MLA_V = 16, 384, 256, 64, 32, 64
ROPE_THETA = 10000.0
N_EXPERTS, N_GROUPS, EPG, D_EXPERT = 16, 4, 4, 512
N_PAIRS = EPG * (EPG - 1) // 2
N_CLASSES = N_GROUPS * N_PAIRS
Q_BLOCK = 128
LN_EPS = 1e-5
RMS_EPS = 1e-6
DEPTH = 2
ALPHA = (2 * DEPTH) ** 0.25

LANES = 128
MXU_N = 256
VMEM_LIMIT = 48 * 1024 * 1024

TOK_TILE = 512
MOE_TILE = 256
CLS_PAD = 32

AB_Q, AB_QR, AB_KR, AB_VR, AB_GR, AB_KV, AB_GATE = 0, 512, 1024, 1536, 2048, 2560, 3328
AB_W = 3584


def _cparams(n_axes):
    return pltpu.CompilerParams(dimension_semantics=("arbitrary",) * n_axes, vmem_limit_bytes=VMEM_LIMIT)


def _split_bf16(w):
    hi = w.astype(BF16)
    return hi, (w - hi.astype(F32)).astype(BF16)


def _dot_split(x, w_hi, w_lo):
    x_hi, x_lo = _split_bf16(x)
    dot = functools.partial(jnp.dot, preferred_element_type=F32)
    return dot(x_hi, w_hi) + (dot(x_lo, w_hi) + dot(x_hi, w_lo))


def _mm_kernel(x_ref, w_ref, *rest, precise_from):
    o_ref = rest[-1]
    step = 2 * MXU_N

    def single_pass():
        xb = x_ref[...].astype(BF16)
        for c in range(0, o_ref.shape[1], step):
            o_ref[:, c:c + step] = jnp.dot(xb, w_ref[:, c:c + step], preferred_element_type=F32).astype(o_ref.dtype)

    if precise_from is None:
        single_pass()
        return
    pl.when(pl.program_id(0) < precise_from)(single_pass)

    @pl.when(pl.program_id(0) >= precise_from)
    def _():
        x = x_ref[...]
        for c in range(0, o_ref.shape[1], step):
            o_ref[:, c:c + step] = _dot_split(x, w_ref[:, c:c + step], rest[0][:, c:c + step]).astype(o_ref.dtype)


def _mm(x, w, w_lo=None, precise_from=None, out_dtype=F32, tile=TOK_TILE):
    n, k = x.shape
    m = w.shape[1]
    assert n % tile == 0 and m % (2 * MXU_N) == 0
    weights = (w,) if precise_from is None else (w, w_lo)
    return pl.pallas_call(
        functools.partial(_mm_kernel, precise_from=precise_from), grid=(n // tile,),
        in_specs=[pl.BlockSpec((tile, k), lambda i: (i, 0))] + [pl.BlockSpec((k, m), lambda i: (0, 0))] * len(weights),
        out_specs=pl.BlockSpec((tile, m), lambda i: (i, 0)),
        out_shape=jax.ShapeDtypeStruct((n, m), out_dtype),
        compiler_params=_cparams(1), name="mm")(x, *weights)


def _layer_norm_rows(z, g, b):
    mu = jnp.mean(z, -1, keepdims=True)
    zc = z - mu
    var = jnp.mean(zc * zc, -1, keepdims=True)
    return zc * lax.rsqrt(var + LN_EPS) * g + b


def _proj_ln_kernel(a_ref, w_ref, *rest, precise_from):
    x_ref, g_ref, b_ref, o_ref = rest[-4:]

    def finish(y):
        o_ref[...] = _layer_norm_rows(ALPHA * x_ref[...] + y, g_ref[...], b_ref[...])

    def single_pass():
        finish(jnp.dot(a_ref[...].astype(BF16), w_ref[...], preferred_element_type=F32))

    if precise_from is None:
        single_pass()
        return
    pl.when(pl.program_id(0) < precise_from)(single_pass)

    @pl.when(pl.program_id(0) >= precise_from)
    def _():
        finish(_dot_split(a_ref[...], w_ref[...], rest[0][...]))


def _proj_ln(a, w, x, g, b, w_lo=None, precise_from=None, tile=TOK_TILE):
    n, k = a.shape
    d = w.shape[1]
    weights = (w,) if precise_from is None else (w, w_lo)
    row = pl.BlockSpec((1, d), lambda i: (0, 0))
    return pl.pallas_call(
        functools.partial(_proj_ln_kernel, precise_from=precise_from), grid=(n // tile,),
        in_specs=[pl.BlockSpec((tile, k), lambda i: (i, 0))] + [pl.BlockSpec((k, d), lambda i: (0, 0))] * len(weights)
        + [pl.BlockSpec((tile, d), lambda i: (i, 0)), row, row],
        out_specs=pl.BlockSpec((tile, d), lambda i: (i, 0)),
        out_shape=jax.ShapeDtypeStruct((n, d), F32),
        compiler_params=_cparams(1), name="proj_ln")(a, *weights, x, g.reshape(1, d), b.reshape(1, d))


def _router_kernel(x_ref, wrt_ref, wrt_lo_ref, bias_ref, tri_ref, cls_ref, rank_ref, cnt_ref, carry_ref):
    @pl.when(pl.program_id(0) == 0)
    def _():
        carry_ref[...] = jnp.zeros_like(carry_ref)

    tile = x_ref.shape[0]
    x_hi, x_lo = _split_bf16(x_ref[...])
    logits = _dot_nt(wrt_ref[...], x_hi) + (_dot_nt(wrt_ref[...], x_lo) + _dot_nt(wrt_lo_ref[...], x_hi))
    ssel = jax.nn.sigmoid(logits) + bias_ref[...]
    rows = [ssel[e:e + 1, :] for e in range(N_EXPERTS)]

    def top2sum(a, b, c, d):
        return jnp.maximum(jnp.maximum(jnp.maximum(a + b, a + c), jnp.maximum(a + d, b + c)),
                           jnp.maximum(b + d, c + d))

    gscore = [top2sum(*rows[EPG * g:EPG * (g + 1)]) for g in range(N_GROUPS)]
    best, gi = gscore[0], jnp.zeros((1, tile), I32)
    for g in range(1, N_GROUPS):
        better = gscore[g] > best
        gi = jnp.where(better, g, gi)
        best = jnp.where(better, gscore[g], best)
    v = []
    for j in range(EPG):
        vj = rows[j]
        for g in range(1, N_GROUPS):
            vj = jnp.where(gi == g, rows[EPG * g + j], vj)
        v.append(vj)
    sel = []
    for i in range(EPG):
        r = jnp.zeros((1, tile), I32)
        for j in range(EPG):
            if j == i:
                continue
            beats = (v[j] > v[i]) | ((v[j] == v[i]) if j < i else False)
            r = r + beats.astype(I32)
        sel.append(r < 2)
    lo = jnp.where(sel[0], 0, jnp.where(sel[1], 1, 2))
    hi = jnp.where(sel[3], 3, jnp.where(sel[2], 2, 1))
    base = jnp.where(lo == 0, 0, jnp.where(lo == 1, 3, 5))
    cls = gi * N_PAIRS + base + hi - lo - 1

    onehot = (lax.broadcasted_iota(I32, (CLS_PAD, tile), 0) == cls).astype(F32)
    prefix = jnp.dot(onehot.astype(BF16), tri_ref[...], preferred_element_type=F32)
    carry = carry_ref[...]
    rank = jnp.sum(onehot * (prefix - 1.0 + carry), axis=0, keepdims=True)
    carry = carry + jnp.sum(onehot, axis=1, keepdims=True)
    carry_ref[...] = carry
    cls_ref[...] = cls
    rank_ref[...] = rank.astype(I32)
    cnt_ref[...] = carry.astype(I32)


def _route(x, wrt, wrt_lo, bias, tile=TOK_TILE):
    n, d = x.shape
    tri = (np.arange(tile)[:, None] <= np.arange(tile)[None, :]).astype(np.float32)
    w_spec = pl.BlockSpec((N_EXPERTS, d), lambda i: (0, 0))
    return pl.pallas_call(
        _router_kernel, grid=(n // tile,),
        in_specs=[pl.BlockSpec((tile, d), lambda i: (i, 0)), w_spec, w_spec,
                  pl.BlockSpec((N_EXPERTS, 1), lambda i: (0, 0)), pl.BlockSpec((tile, tile), lambda i: (0, 0))],
        out_specs=[pl.BlockSpec((1, tile), lambda i: (0, i)), pl.BlockSpec((1, tile), lambda i: (0, i)),
                   pl.BlockSpec((CLS_PAD, 1), lambda i: (0, 0))],
        out_shape=[jax.ShapeDtypeStruct((1, n), I32), jax.ShapeDtypeStruct((1, n), I32),
                   jax.ShapeDtypeStruct((CLS_PAD, 1), I32)],
        scratch_shapes=[pltpu.VMEM((CLS_PAD, 1), F32)],
        compiler_params=_cparams(1), name="moe_route")(x, wrt, wrt_lo, bias, jnp.asarray(tri, BF16))


def _row_copy(src_ref, src_row, dst_ref, dst_row, sem):
    return pltpu.make_async_copy(src_ref.at[pl.ds(src_row, 1)], dst_ref.at[pl.ds(dst_row, 1)], sem)


ROW_DMA_UNROLL = 8


def _slot_kernel(off_ref, cls_ref, rank_ref, slot_ref):
    cls = cls_ref[...]
    slot = rank_ref[...]
    for c in range(N_CLASSES):
        slot = slot + jnp.where(cls == c, off_ref[c], 0)
    slot_ref[...] = slot


def _slots(cls, rank, off):
    row = pl.BlockSpec(cls.shape, lambda i, off: (0, 0))
    return pl.pallas_call(
        _slot_kernel,
        grid_spec=pltpu.PrefetchScalarGridSpec(num_scalar_prefetch=1, grid=(1,), in_specs=[row, row], out_specs=row),
        out_shape=jax.ShapeDtypeStruct(cls.shape, I32),
        compiler_params=_cparams(1), name="moe_slots")(off, cls, rank)


def _scatter_kernel(slot_ref, x_ref, xs_in_ref, xs_ref, sem):
    del xs_in_ref
    tile = x_ref.shape[0]

    def start(r, c):
        _row_copy(x_ref, r, xs_ref, slot_ref[0, r], sem).start()
        return c

    lax.fori_loop(0, tile, start, 0, unroll=ROW_DMA_UNROLL)

    def wait(r, c):
        _row_copy(x_ref, 0, xs_ref, 0, sem).wait()
        return c

    lax.fori_loop(0, tile, wait, 0, unroll=ROW_DMA_UNROLL)


def _scatter_rows(x, slot, n_slots, tile=MOE_TILE):
    n, d = x.shape
    return pl.pallas_call(
        _scatter_kernel, grid=(n // tile,),
        in_specs=[pl.BlockSpec((1, tile), lambda i: (0, i), memory_space=pltpu.SMEM),
                  pl.BlockSpec((tile, d), lambda i: (i, 0)), pl.BlockSpec(memory_space=pl.ANY)],
        out_specs=pl.BlockSpec(memory_space=pl.ANY),
        scratch_shapes=[pltpu.SemaphoreType.DMA(())],
        out_shape=jax.ShapeDtypeStruct((n_slots, d), F32),
        input_output_aliases={2: 0},
        compiler_params=_cparams(1), name="moe_scatter")(slot, x, jnp.zeros((n_slots, d), F32))


def _expert_kernel(e1_ref, e2_ref, valid_ref, xs_ref, wr_ref, g1_ref, u1_ref, d1_ref, g2_ref, u2_ref, d2_ref,
                   o_ref):
    j = pl.program_id(0)

    @pl.when(valid_ref[j] == 0)
    def _():
        o_ref[...] = jnp.zeros_like(o_ref)

    @pl.when(valid_ref[j] != 0)
    def _():
        xb = xs_ref[...].astype(BF16)
        s = jax.nn.sigmoid(jnp.dot(xb, wr_ref[...], preferred_element_type=F32))
        lane = lax.broadcasted_iota(I32, s.shape, 1)
        w1 = jnp.sum(jnp.where(lane == e1_ref[j], s, 0.0), axis=1, keepdims=True)
        w2 = jnp.sum(jnp.where(lane == e2_ref[j], s, 0.0), axis=1, keepdims=True)
        tot = w1 + w2

        def mlp(g_ref, u_ref, d_ref):
            hg = jnp.dot(xb, g_ref[0], preferred_element_type=F32)
            hu = jnp.dot(xb, u_ref[0], preferred_element_type=F32)
            hdn = hg * jax.nn.sigmoid(hg) * hu
            return jnp.dot(hdn.astype(BF16), d_ref[0], preferred_element_type=F32)

        o_ref[...] = (w1 / tot) * mlp(g1_ref, u1_ref, d1_ref) + (w2 / tot) * mlp(g2_ref, u2_ref, d2_ref)


def _expert_pairs(xs, wr, wg, wu, wd, e1, e2, valid, tile=MOE_TILE):
    n_slots, d = xs.shape
    de = wg.shape[2]
    up = lambda sel: pl.BlockSpec((1, d, de), lambda j, e1, e2, v: ((e1, e2)[sel][j], 0, 0))
    down = lambda sel: pl.BlockSpec((1, de, d), lambda j, e1, e2, v: ((e1, e2)[sel][j], 0, 0))
    return pl.pallas_call(
        _expert_kernel,
        grid_spec=pltpu.PrefetchScalarGridSpec(
            num_scalar_prefetch=3, grid=(n_slots // tile,),
            in_specs=[pl.BlockSpec((tile, d), lambda j, e1, e2, v: (j, 0)),
                      pl.BlockSpec((d, N_EXPERTS), lambda j, e1, e2, v: (0, 0)),
                      up(0), up(0), down(0), up(1), up(1), down(1)],
            out_specs=pl.BlockSpec((tile, d), lambda j, e1, e2, v: (j, 0))),
        out_shape=jax.ShapeDtypeStruct((n_slots, d), F32),
        compiler_params=_cparams(1), name="moe_experts")(e1, e2, valid, xs, wr, wg, wu, wd, wg, wu, wd)


def _gather_ln_kernel(slot_ref, next_ref, x_ref, ys_ref, g_ref, b_ref, o_ref, ybuf, sem):
    tile = x_ref.shape[0]
    i = pl.program_id(0)
    cur = i % 2

    def fetch(idx_ref, buf_slot):
        def start(r, c):
            _row_copy(ys_ref, idx_ref[0, r], ybuf.at[buf_slot], r, sem.at[buf_slot]).start()
            return c
        lax.fori_loop(0, tile, start, 0, unroll=ROW_DMA_UNROLL)

    @pl.when(i == 0)
    def _():
        fetch(slot_ref, cur)

    @pl.when(i + 1 < pl.num_programs(0))
    def _():
        fetch(next_ref, 1 - cur)

    def wait(r, c):
        _row_copy(ys_ref, 0, ybuf.at[cur], 0, sem.at[cur]).wait()
        return c

    lax.fori_loop(0, tile, wait, 0, unroll=ROW_DMA_UNROLL)
    o_ref[...] = _layer_norm_rows(ALPHA * x_ref[...] + ybuf[cur], g_ref[...], b_ref[...])


def _gather_ln(x, ys, slot, g, b, tile=MOE_TILE):
    n, d = x.shape
    n_tiles = n // tile
    vec = pl.BlockSpec((1, d), lambda i: (0, 0))
    return pl.pallas_call(
        _gather_ln_kernel, grid=(n_tiles,),
        in_specs=[pl.BlockSpec((1, tile), lambda i: (0, i), memory_space=pltpu.SMEM),
                  pl.BlockSpec((1, tile), lambda i: (0, jnp.minimum(i + 1, n_tiles - 1)), memory_space=pltpu.SMEM),
                  pl.BlockSpec((tile, d), lambda i: (i, 0)), pl.BlockSpec(memory_space=pl.ANY), vec, vec],
        out_specs=pl.BlockSpec((tile, d), lambda i: (i, 0)),
        scratch_shapes=[pltpu.VMEM((2, tile, d), F32), pltpu.SemaphoreType.DMA((2,))],
        out_shape=jax.ShapeDtypeStruct((n, d), F32),
        compiler_params=_cparams(1), name="moe_gather_ln")(slot, slot, x, ys, g.reshape(1, d), b.reshape(1, d))


_PAIR_LO = np.array([0, 0, 0, 1, 1, 2], np.int32)
_PAIR_HI = np.array([1, 2, 3, 2, 3, 3], np.int32)


def _moe_ln(x, w_router, router_bias, wg, wu, wd, g, b):
    n, d = x.shape
    cls, rank, cnt = _route(x, *_split_bf16(w_router.T), router_bias.reshape(N_EXPERTS, 1).astype(F32))
    cnt = cnt[:N_CLASSES, 0]
    tiles = (cnt + MOE_TILE - 1) // MOE_TILE
    tile_end = jnp.cumsum(tiles)
    off = jnp.zeros((CLS_PAD,), I32).at[:N_CLASSES].set((tile_end - tiles) * MOE_TILE)
    n_tiles = n // MOE_TILE + N_CLASSES
    tile_id = jnp.arange(n_tiles, dtype=I32)
    tile_cls = jnp.minimum(jnp.sum(tile_id[:, None] >= tile_end[None, :], axis=1), N_CLASSES - 1).astype(I32)
    valid = (tile_id < tile_end[-1]).astype(I32)
    grp, pair = tile_cls // N_PAIRS, tile_cls % N_PAIRS
    e1 = grp * EPG + jnp.asarray(_PAIR_LO)[pair]
    e2 = grp * EPG + jnp.asarray(_PAIR_HI)[pair]
    slot = _slots(cls, rank, off)
    xs = _scatter_rows(x, slot, n_tiles * MOE_TILE)
    ys = _expert_pairs(xs, w_router.astype(BF16), wg, wu, wd, e1, e2, valid)
    return _gather_ln(x, ys, slot, g, b)


def _prep_w_in_ab(w):
    k = w.shape[0]
    q, kv, gate, qr, kr, vr, gr = jnp.split(w, np.cumsum([512, 768, 24, 512, 512, 512]).tolist(), axis=1)
    pad = jnp.zeros((k, AB_W - AB_GATE - 24), w.dtype)
    return _split_bf16(jnp.concatenate([q, qr, kr, vr, gr, kv, gate, pad], axis=1))


CMP_GROUP = 4


def _prep_compress(w1, w2, pe, precise):
    hd = NSA_HD
    w1r = w1.reshape(2, 2, CMP_STRIDE, hd, CMP_HIDDEN)
    w1bd = jnp.zeros((2, 2, CMP_STRIDE, NSA_G * hd, NSA_G * CMP_HIDDEN), F32)
    w2bd = jnp.zeros((2, NSA_G * CMP_HIDDEN, NSA_G * hd), F32)
    for g in range(NSA_G):
        w1bd = w1bd.at[..., g * hd:(g + 1) * hd, g * CMP_HIDDEN:(g + 1) * CMP_HIDDEN].set(w1r)
        w2bd = w2bd.at[:, g * CMP_HIDDEN:(g + 1) * CMP_HIDDEN, g * hd:(g + 1) * hd].set(w2)
    w1cat = jnp.concatenate([w1bd[:, 0], w1bd[:, 1]], axis=-1)
    if precise:
        stack = lambda w: jnp.concatenate([_split_bf16(w)[0]] * 2 + [_split_bf16(w)[1]], axis=-2)
    else:
        stack = lambda w: w.astype(BF16)
    w1s = stack(w1cat)
    w1s = w1s.reshape(2, CMP_STRIDE // CMP_GROUP, CMP_GROUP * w1s.shape[2], w1s.shape[3])
    per = pe.reshape(2, 2, CMP_STRIDE, hd)
    pe_l = jnp.concatenate([per] * NSA_G, axis=-1)
    return w1s, stack(w2bd), pe_l


def _split_lhs(x, precise):
    if not precise:
        return x.astype(BF16)
    hi, lo = _split_bf16(x)
    return jnp.concatenate([hi, lo, hi], axis=1)


def _compress_rows(x_ref, kind, n_chunks, pe_ref, w1_ref, w2_ref, precise=False):
    width = NSA_G * CMP_HIDDEN
    pe_pad = 16
    acc = jnp.zeros((n_chunks + 2 * pe_pad, 2 * width), F32)
    for grp in range(CMP_STRIDE // CMP_GROUP):
        blocks = []
        for s in range(grp * CMP_GROUP, (grp + 1) * CMP_GROUP):
            xs = x_ref[pl.ds(s, n_chunks, stride=CMP_STRIDE), :]
            pe_rows = [jnp.broadcast_to(pe_ref[kind, j, s:s + 1, :], (pe_pad, xs.shape[1])) for j in range(2)]
            blocks.append(jnp.concatenate([_split_lhs(xs, precise)] + [_split_lhs(r, precise) for r in pe_rows],
                                          axis=0))
        acc = acc + jnp.dot(jnp.concatenate(blocks, axis=1), w1_ref[kind, grp], preferred_element_type=F32)
    first = acc[0:n_chunks, 0:width] + acc[n_chunks:n_chunks + 1, 0:width]
    second = acc[0:n_chunks, width:] + acc[n_chunks + pe_pad:n_chunks + pe_pad + 1, width:]
    hdn = jax.nn.gelu(first + pltpu.roll(second, n_chunks - 1, 0))
    return jnp.dot(_split_lhs(hdn, precise), w2_ref[kind], preferred_element_type=F32)


def _dot_split2(a, b, nt=False):
    (a_hi, a_lo), (b_hi, b_lo) = _split_bf16(a), _split_bf16(b)
    dot = _dot_nt if nt else functools.partial(jnp.dot, preferred_element_type=F32)
    return dot(a_hi, b_hi) + (dot(a_lo, b_hi) + dot(a_hi, b_lo))


def _compress_kernel(k_ref, v_ref, pe_ref, w1_ref, w2_ref, o_ref):
    n_chunks = o_ref.shape[1]
    o_ref[0, :, 0:LANES] = _compress_rows(k_ref, 0, n_chunks, pe_ref, w1_ref, w2_ref)
    o_ref[0, :, LANES:2 * LANES] = _compress_rows(v_ref, 1, n_chunks, pe_ref, w1_ref, w2_ref)


def _compress_prompt(h, batch, seq, w1bd, w2bd, pe_l):
    n_chunks = seq // CMP_STRIDE
    full = lambda a: pl.BlockSpec(a.shape, lambda b: (0,) * a.ndim)
    return pl.pallas_call(
        _compress_kernel, grid=(batch,),
        in_specs=[pl.BlockSpec((seq, LANES), lambda b: (b, AB_KV // LANES)),
                  pl.BlockSpec((seq, LANES), lambda b: (b, AB_KV // LANES + 1)), full(pe_l), full(w1bd), full(w2bd)],
        out_specs=pl.BlockSpec((1, n_chunks, 256), lambda b: (b, 0, 0)),
        out_shape=jax.ShapeDtypeStruct((batch, n_chunks, 256), F32),
        compiler_params=_cparams(1), name="nsa_compress")(h, h, pe_l, w1bd, w2bd)


def _masked_softmax(s, mask):
    s = jnp.where(mask, s, -jnp.inf)
    m = jnp.max(s, -1, keepdims=True)
    m = jnp.where(m == -jnp.inf, 0.0, m)
    e = jnp.exp(s - m)
    den = jnp.maximum(jnp.sum(e, -1, keepdims=True), jnp.finfo(F32).tiny)
    return e * (1.0 / den)


def _dot_nt(a, b):
    return lax.dot_general(a, b, (((1,), (1,)), ((), ())), preferred_element_type=F32)


def _topn_rows(imp_t, n_top):
    n_blocks = imp_t.shape[0]
    blk = lax.broadcasted_iota(I32, imp_t.shape, 0)
    rank = jnp.zeros(imp_t.shape, I32)
    for j in range(n_blocks):
        row = imp_t[j:j + 1, :]
        rank = rank + ((row > imp_t) | ((row == imp_t) & (blk > j))).astype(I32)
    return rank < n_top


NSA_SEL_CHUNK = 256


def _nsa_prompt_kernel(q_ref, gate_ref, slc_ref, win_ref, kvc_ref, selmap_ref, o_ref, *, seq, win_span):
    qb = q_ref.shape[0]
    i0 = pl.program_id(1) * qb
    n_cmp_pad = kvc_ref.shape[1]
    n_slc = seq // SEL_BLOCK
    ck = min(NSA_SEL_CHUNK, seq)
    scale = NSA_HD ** -0.5
    qpos = i0 + lax.broadcasted_iota(I32, (qb, 1), 0)
    lane = lax.broadcasted_iota(I32, (qb, LANES), 1)
    gates = jax.nn.sigmoid(gate_ref[:, 0:LANES])

    kvc = kvc_ref[0].astype(BF16)
    cmp_last = lax.broadcasted_iota(I32, (1, n_cmp_pad), 1) * CMP_STRIDE + (CMP_BLOCK - 1)
    cmp_mask = cmp_last <= qpos
    w0 = jnp.maximum(i0 + qb - win_span, 0)
    w0 = pl.multiple_of(w0, qb)
    wpos = w0 + lax.broadcasted_iota(I32, (1, win_span), 1)
    wmask = (wpos <= qpos) & (wpos > qpos - NSA_WINDOW)
    k_win = win_ref[pl.ds(w0, win_span), 0:LANES].astype(BF16)
    v_win = win_ref[pl.ds(w0, win_span), LANES:2 * LANES].astype(BF16)
    blk_t = lax.broadcasted_iota(I32, (n_slc, qb), 0)
    qpos_t = i0 + lax.broadcasted_iota(I32, (n_slc, qb), 1)
    cur_t = qpos_t // SEL_BLOCK
    forced_t = (blk_t == 0) | ((blk_t <= cur_t) & (blk_t > cur_t - SEL_LOCAL))
    valid_t = blk_t * SEL_BLOCK <= qpos_t
    chunk_blk = lax.broadcasted_iota(I32, (n_slc, ck), 1) // SEL_BLOCK
    chunk_row = lax.broadcasted_iota(I32, (n_slc, ck), 0)
    chunk_pos = lax.broadcasted_iota(I32, (1, ck), 1)
    n_chunks = (i0 + qb - 1) // ck + 1

    out_tiles = [None] * (NSA_H // 2)
    for g in range(NSA_G):
        heads = list(range(g * NSA_HPG, (g + 1) * NSA_HPG))
        q128, p_cmp = {}, {}
        imp_t = jnp.zeros((n_slc, qb), F32)
        for h in heads:
            t = q_ref[:, (h // 2) * LANES:(h // 2 + 1) * LANES] * scale
            t = jnp.where((lane // NSA_HD) == (h % 2), t, 0.0)
            if h % 2 != g:
                t = pltpu.roll(t, NSA_HD, 1)
            q128[h] = t.astype(BF16)
            p = _masked_softmax(_dot_nt(q128[h], kvc[:, 0:LANES]), cmp_mask)
            p_cmp[h] = p.astype(BF16)
            imp_t = imp_t + _dot_nt(selmap_ref[...], p_cmp[h])
        imp_t = jnp.where(forced_t, FORCE_SCORE, imp_t)
        imp_t = jnp.where(valid_t, imp_t, -jnp.inf)
        sel_t = _topn_rows(imp_t, min(SEL_TOPN, n_slc)).astype(BF16)

        def sel_step(kc, carry):
            rows = pl.ds(pl.multiple_of(kc * ck, ck), ck)
            k = slc_ref[rows, 0:LANES].astype(BF16)
            v = slc_ref[rows, LANES:2 * LANES].astype(BF16)
            in_chunk = (chunk_row == kc * (ck // SEL_BLOCK) + chunk_blk).astype(BF16)
            picked = lax.dot_general(sel_t, in_chunk, (((0,), (0,)), ((), ())), preferred_element_type=F32)
            kmask = (picked > 0.5) & (kc * ck + chunk_pos <= qpos)
            new = []
            for h, (m, l, acc) in zip(heads, carry):
                s = jnp.where(kmask, _dot_nt(q128[h], k), -jnp.inf)
                m_new = jnp.maximum(m, jnp.max(s, -1, keepdims=True))
                a = jnp.exp(m - m_new)
                e = jnp.exp(s - m_new)
                new.append((m_new, a * l + jnp.sum(e, -1, keepdims=True),
                            a * acc + jnp.dot(e.astype(BF16), v, preferred_element_type=F32)))
            return tuple(new)

        init = tuple((jnp.full((qb, 1), -jnp.inf, F32), jnp.zeros((qb, 1), F32), jnp.zeros((qb, LANES), F32))
                     for _ in heads)
        sel_state = lax.fori_loop(0, n_chunks, sel_step, init)
        for h, (m, l, acc) in zip(heads, sel_state):
            o_cmp = jnp.dot(p_cmp[h], kvc[:, LANES:2 * LANES], preferred_element_type=F32)
            o_slc = acc * (1.0 / l)
            p = _masked_softmax(_dot_nt(q128[h], k_win), wmask)
            o_win = jnp.dot(p.astype(BF16), v_win, preferred_element_type=F32)
            o = (gates[:, 3 * h:3 * h + 1] * o_cmp + gates[:, 3 * h + 1:3 * h + 2] * o_slc
                 + gates[:, 3 * h + 2:3 * h + 3] * o_win)
            if h % 2 != g:
                o = pltpu.roll(o, NSA_HD, 1)
            keep = (lane // NSA_HD) == (h % 2)
            prev = out_tiles[h // 2]
            out_tiles[h // 2] = jnp.where(keep, o, 0.0 if prev is None else prev)
    for t, tile in enumerate(out_tiles):
        o_ref[:, t * LANES:(t + 1) * LANES] = tile


def _nsa_prompt(h, kvc, batch, seq):
    nqb = seq // Q_BLOCK
    n_cmp_pad = seq // CMP_STRIDE
    n_slc = seq // SEL_BLOCK
    assert seq % max(Q_BLOCK, min(NSA_SEL_CHUNK, seq)) == 0 and n_slc % 8 == 0 and n_slc >= SEL_LOCAL + 1
    win_span = min(NSA_WINDOW + Q_BLOCK, seq)
    sj = np.arange(n_slc)[:, None] * SEL_BLOCK
    ci = np.arange(n_cmp_pad)[None, :] * CMP_STRIDE
    selmap = np.clip(np.minimum(ci + CMP_BLOCK, sj + SEL_BLOCK) - np.maximum(ci, sj), 0, None) / CMP_STRIDE
    selmap[:, n_cmp_pad - 1:] = 0.0
    const = lambda a: pl.BlockSpec(a.shape, lambda b, i: (0,) * a.ndim)
    selmap = jnp.asarray(selmap, BF16)
    return pl.pallas_call(
        functools.partial(_nsa_prompt_kernel, seq=seq, win_span=win_span), grid=(batch, nqb),
        in_specs=[pl.BlockSpec((Q_BLOCK, 512), lambda b, i: (b * nqb + i, AB_Q // 512)),
                  pl.BlockSpec((Q_BLOCK, 256), lambda b, i: (b * nqb + i, AB_GATE // 256)),
                  pl.BlockSpec((seq, 256), lambda b, i: (b, (AB_KV + 256) // 256)),
                  pl.BlockSpec((seq, 256), lambda b, i: (b, (AB_KV + 512) // 256)),
                  pl.BlockSpec((1, n_cmp_pad, 256), lambda b, i: (b, 0, 0)),
                  const(selmap)],
        out_specs=pl.BlockSpec((Q_BLOCK, 512), lambda b, i: (b * nqb + i, 0)),
        out_shape=jax.ShapeDtypeStruct((batch * seq, 512), F32),
        compiler_params=_cparams(2), name="nsa_prompt")(h, h, h, h, kvc, selmap)


def _rope_tables(pos, half):
    inv = ROPE_THETA ** (-jnp.arange(half, dtype=F32) / half)
    ang = pos.astype(F32)[:, None] * inv[None, :]
    cos, sin = jnp.cos(ang), jnp.sin(ang)
    return jnp.concatenate([cos, cos], -1), jnp.concatenate([-sin, sin], -1)


def _retention_tables(chunk):
    log_g = jnp.log1p(-jnp.exp2(-5.0 - jnp.arange(RET_H, dtype=F32)))
    t = jnp.arange(chunk, dtype=F32)
    diff = t[:, None] - t[None, :]
    decay = jnp.where(diff >= 0, jnp.exp(log_g[:, None, None] * jnp.maximum(diff, 0.0)), 0.0)
    xi = jnp.exp(log_g[:, None] * (t + 1.0))
    zeta = jnp.exp(log_g[:, None] * (chunk - 1.0 - t))
    g_chunk = jnp.broadcast_to(jnp.exp(log_g * chunk)[:, None], (RET_H, chunk))
    cols = jnp.stack([xi, zeta, g_chunk], axis=-1)
    return decay, jnp.pad(cols, ((0, 0), (0, 0), (0, LANES - 3)))


def _head_norm_gate(o, gate):
    mu = jnp.mean(o, -1, keepdims=True)
    oc = o - mu
    var = jnp.mean(oc * oc, -1, keepdims=True)
    return oc * lax.rsqrt(var + LN_EPS) * (gate * jax.nn.sigmoid(gate))


def _rope_rows(x, cos, sin):
    return x * cos + pltpu.roll(x, x.shape[-1] // 2, 1) * sin


def _ret_prompt_kernel(q_ref, k_ref, v_ref, g_ref, cos_ref, sin_ref, decay_ref, cols_ref, o_ref, s_ref, *, chunk):
    n_chunks = q_ref.shape[0] // chunk
    s_ref[...] = jnp.zeros_like(s_ref)

    def body(c, carry):
        r = pl.ds(pl.multiple_of(c * chunk, chunk), chunk)
        cos, sin = cos_ref[r, :], sin_ref[r, :]
        for hh in range(RET_H):
            cols = slice(hh * LANES, (hh + 1) * LANES)
            xi, zeta, g_chunk = cols_ref[hh, :, 0:1], cols_ref[hh, :, 1:2], cols_ref[hh, 0:1, 2:3]
            state = s_ref[0, hh]
            q = _rope_rows(q_ref[r, cols], cos, sin)
            k = _rope_rows(k_ref[r, cols], cos, sin) * (RET_DK ** -0.5)
            v = v_ref[r, cols].astype(BF16)
            inner = _dot_nt(q.astype(BF16), k.astype(BF16)) * decay_ref[hh]
            o = (jnp.dot(inner.astype(BF16), v, preferred_element_type=F32)
                 + jnp.dot((q * xi).astype(BF16), state.astype(BF16), preferred_element_type=F32))
            o_ref[r, cols] = _head_norm_gate(o, g_ref[r, cols])
            kz = (k * zeta).astype(BF16)
            s_ref[0, hh] = g_chunk * state + lax.dot_general(kz, v, (((0,), (0,)), ((), ())),
                                                             preferred_element_type=F32)
        return carry

    lax.fori_loop(0, n_chunks, body, 0)


def _ret_prompt(h, batch, seq):
    chunk = math.gcd(seq, RET_CHUNK)
    assert chunk == RET_CHUNK and RET_DK == LANES and RET_DV == LANES
    cos, sin = _rope_tables(jnp.arange(seq), RET_DK // 2)
    decay, cols = _retention_tables(chunk)
    width = RET_H * LANES
    col = lambda base: pl.BlockSpec((seq, width), lambda b: (b, base // width))
    const = lambda a: pl.BlockSpec(a.shape, lambda b: (0,) * a.ndim)
    return pl.pallas_call(
        functools.partial(_ret_prompt_kernel, chunk=chunk), grid=(batch,),
        in_specs=[col(AB_QR), col(AB_KR), col(AB_VR), col(AB_GR), const(cos), const(sin), const(decay), const(cols)],
        out_specs=[pl.BlockSpec((seq, width), lambda b: (b, 0)),
                   pl.BlockSpec((1, RET_H, RET_DK, RET_DV), lambda b: (b, 0, 0, 0))],
        out_shape=[jax.ShapeDtypeStruct((batch * seq, RET_H * RET_DV), F32),
                   jax.ShapeDtypeStruct((batch, RET_H, RET_DK, RET_DV), F32)],
        compiler_params=_cparams(1), name="ret_prompt")(h, h, h, h, cos, sin, decay, cols)


def _dev_mixer_ab_prompt(x, w_in, w_out, w_cmp1, w_cmp2, cmp_pe):
    batch, seq, d = x.shape
    h = _mm(x.reshape(batch * seq, d), _prep_w_in_ab(w_in)[0])
    w1bd, w2bd, pe_l = _prep_compress(w_cmp1, w_cmp2, cmp_pe, precise=False)
    kvc = _compress_prompt(h, batch, seq, w1bd, w2bd, pe_l)
    o_nsa = _nsa_prompt(h, kvc, batch, seq)
    o_ret, s_new = _ret_prompt(h, batch, seq)
    mixed = jnp.concatenate([o_nsa, o_ret], axis=1)
    mix = _mm(mixed, w_out.astype(BF16))
    new_kv = h[:, AB_KV:AB_KV + 512].reshape(batch, seq, 4, NSA_G, NSA_HD)
    n_keep = min(NSA_WINDOW, seq)
    win = h[:, AB_KV + 512:AB_KV + 768].reshape(batch, seq, 2, NSA_G, NSA_HD)[:, seq - n_keep:]
    return mix.reshape(batch, seq, d), new_kv, win, s_new


def _page_copy(cache_ref, page, col0, width, dst_ref, sem):
    return pltpu.make_async_copy(cache_ref.at[page, :, pl.ds(col0, width)], dst_ref, sem)


def _topn_ids(imp, n_blocks, n_top, out_lane, lane0):
    lane = lax.broadcasted_iota(I32, imp.shape, 1)
    taken = lane >= n_blocks
    ids = jnp.zeros(out_lane.shape, I32)
    for t in range(n_top):
        m = jnp.max(jnp.where(taken, -jnp.inf, imp), axis=1, keepdims=True)
        idx = jnp.min(jnp.where(jnp.logical_not(taken) & (imp >= m), lane.astype(F32), float(imp.shape[1])),
                      axis=1, keepdims=True).astype(I32)
        ids = jnp.where(out_lane == lane0 + t, idx, ids)
        taken = taken | (lane == idx)
    return ids


def _nsa_sample_cmp_kernel(tbl_ref, cache_ref, q_ref, pe_ref, w1_ref, w2_ref, selmap_ref,
                           ocmp_ref, ids_ref,
                           buf, rows, sem, *, n_pages, past):
    b = pl.program_id(0)
    nb = pl.num_programs(0)
    slot = b % 2

    def page(bb, p, sl):
        return pltpu.make_async_copy(cache_ref.at[tbl_ref[bb * n_pages + p], pl.ds(0, 2 * LANES)], buf.at[sl, p],
                                     sem.at[sl])

    def fetch(bb, sl):
        def body(p, c):
            page(bb, p, sl).start()
            return c
        lax.fori_loop(0, n_pages, body, 0)

    @pl.when(b == 0)
    def _():
        fetch(b, slot)

    @pl.when(b + 1 < nb)
    def _():
        fetch(b + 1, 1 - slot)

    def wait(p, c):
        page(b, 0, slot).wait()
        return c
    lax.fori_loop(0, n_pages, wait, 0)

    def to_token_major(p, c):
        r = pl.ds(pl.multiple_of(p * PAGE, PAGE), PAGE)
        rows[0, r, :] = buf[slot, p, 0:LANES, :].T
        rows[1, r, :] = buf[slot, p, LANES:2 * LANES, :].T
        return c
    lax.fori_loop(0, n_pages, to_token_major, 0, unroll=4)

    n_chunks = past // CMP_STRIDE
    n_cmp = n_chunks - CMP_BLOCK // CMP_STRIDE + 1
    n_slc = past // SEL_BLOCK + 1
    k_cmp = _compress_rows(rows.at[0], 0, n_chunks, pe_ref, w1_ref, w2_ref, precise=True)
    v_cmp = _compress_rows(rows.at[1], 1, n_chunks, pe_ref, w1_ref, w2_ref, precise=True)
    q = q_ref[0] * (NSA_HD ** -0.5)
    cmp_idx = lax.broadcasted_iota(I32, (1, n_chunks), 1)
    cmp_mask = (cmp_idx < n_cmp) & (cmp_idx * CMP_STRIDE + (CMP_BLOCK - 1) <= past)
    p = _masked_softmax(_dot_split2(q, k_cmp, nt=True), cmp_mask)
    ocmp_ref[0] = _dot_split2(p, v_cmp)
    p_hi, p_lo = _split_bf16(p)
    imp_h = (jnp.dot(p_hi, selmap_ref[...], preferred_element_type=F32)
             + jnp.dot(p_lo, selmap_ref[...], preferred_element_type=F32))
    lane = lax.broadcasted_iota(I32, (1, imp_h.shape[1]), 1)
    cur = past // SEL_BLOCK
    forced = (lane == 0) | ((lane <= cur) & (lane > cur - SEL_LOCAL))
    out_lane = lax.broadcasted_iota(I32, (1, LANES), 1)
    ids = jnp.zeros((1, LANES), I32)
    for g in range(NSA_G):
        imp = jnp.sum(imp_h[g * NSA_HPG:(g + 1) * NSA_HPG], axis=0, keepdims=True)
        imp = jnp.where(forced, FORCE_SCORE, imp)
        imp = jnp.where(lane * SEL_BLOCK <= past, imp, -jnp.inf)
        ids = ids + _topn_ids(imp, n_slc, min(SEL_TOPN, n_slc), out_lane, g * SEL_TOPN)
    ids_ref[0] = ids


def _nsa_sample_cmp(table, cache, q128, w1bd, w2bd, pe_l, past):
    batch, n_pages = table.shape
    n_chunks = past // CMP_STRIDE
    n_slc = past // SEL_BLOCK + 1
    n_slc_pad = -(-n_slc // LANES) * LANES
    ci = np.arange(n_chunks)[:, None] * CMP_STRIDE
    sj = np.arange(n_slc_pad)[None, :] * SEL_BLOCK
    selmap = np.clip(np.minimum(ci + CMP_BLOCK, sj + SEL_BLOCK) - np.maximum(ci, sj), 0, None) / CMP_STRIDE
    selmap[n_chunks - 1:, :] = 0.0
    selmap[:, n_slc:] = 0.0
    selmap = jnp.asarray(selmap, BF16)
    const = lambda a: pl.BlockSpec(a.shape, lambda b, t: (0,) * a.ndim)
    return pl.pallas_call(
        functools.partial(_nsa_sample_cmp_kernel, n_pages=n_pages, past=past),
        grid_spec=pltpu.PrefetchScalarGridSpec(
            num_scalar_prefetch=1, grid=(batch,),
            in_specs=[pl.BlockSpec(memory_space=pl.ANY), pl.BlockSpec((1, NSA_H, LANES), lambda b, t: (b, 0, 0)),
                      const(pe_l), const(w1bd), const(w2bd), const(selmap)],
            out_specs=[pl.BlockSpec((1, NSA_H, LANES), lambda b, t: (b, 0, 0)),
                       pl.BlockSpec((1, 1, LANES), lambda b, t: (b, 0, 0))],
            scratch_shapes=[pltpu.VMEM((2, n_pages, 2 * LANES, PAGE), F32), pltpu.VMEM((2, past, LANES), F32),
                            pltpu.SemaphoreType.DMA((2,))]),
        out_shape=[jax.ShapeDtypeStruct((batch, NSA_H, LANES), F32), jax.ShapeDtypeStruct((batch, 1, LANES), I32)],
        compiler_params=_cparams(1), name="nsa_sample_cmp")(
            table.reshape(-1), cache, q128, pe_l, w1bd, w2bd, selmap)


def _nsa_sample_attend_kernel(tbl_ref, ids_ref, cache_ref, q_ref, ocmp_ref, gate_ref, new_ref, win_ref,
                              o_ref, wout_ref, sbuf, sem, *, n_pages, past):
    b = pl.program_id(0)
    nb = pl.num_programs(0)
    slot = b % 2
    n_sel = NSA_G * SEL_TOPN
    last_blk = past // SEL_BLOCK
    half = SEL_BLOCK
    per_page = PAGE // SEL_BLOCK

    def block_copy(bb, i, sl):
        blk = jnp.minimum(ids_ref[bb * n_sel + i], last_blk - 1)
        page = tbl_ref[bb * n_pages + blk // per_page]
        return pltpu.make_async_copy(cache_ref.at[page, pl.ds(2 * LANES, 2 * LANES)], sbuf.at[sl, i], sem.at[sl])

    def fetch(bb, sl):
        for i in range(n_sel):
            block_copy(bb, i, sl).start()

    @pl.when(b == 0)
    def _():
        fetch(b, slot)

    @pl.when(b + 1 < nb)
    def _():
        fetch(b + 1, 1 - slot)

    n_win = win_ref.shape[2]
    for i in range(n_sel):
        block_copy(b, i, slot).wait()

    q = (q_ref[0] * (NSA_HD ** -0.5))
    new = new_ref[0]
    gates = jax.nn.sigmoid(gate_ref[0])
    row = lax.broadcasted_iota(I32, (NSA_H, 1), 0)

    def attend(s, mask, v, s_new, v_new, new_ok):
        s = jnp.where(mask, s, -jnp.inf)
        s_new = jnp.where(new_ok, s_new, -jnp.inf)
        m = jnp.maximum(jnp.max(s, -1, keepdims=True), s_new)
        m = jnp.where(m == -jnp.inf, 0.0, m)
        e, e_new = jnp.exp(s - m), jnp.exp(s_new - m)
        den = jnp.maximum(jnp.sum(e, -1, keepdims=True) + e_new, jnp.finfo(F32).tiny)
        return (_dot_split2(e, v) + e_new * v_new) * (1.0 / den)

    o_slc = jnp.zeros((NSA_H, LANES), F32)
    key_lane = lax.broadcasted_iota(I32, (1, SEL_TOPN * PAGE), 1)
    s_new = jnp.sum(q * new[:, 0:LANES], axis=-1, keepdims=True)
    for g in range(NSA_G):
        kv = jnp.concatenate([sbuf[slot, g * SEL_TOPN + i].T for i in range(SEL_TOPN)], axis=0)
        ok = jnp.zeros((1, SEL_TOPN * PAGE), jnp.bool_)
        has_new = False
        for i in range(SEL_TOPN):
            blk = ids_ref[b * n_sel + g * SEL_TOPN + i]
            in_block = (key_lane // PAGE == i) & ((key_lane % PAGE) // half == blk % per_page)
            ok = ok | (in_block & (blk != last_blk))
            has_new = has_new | (blk == last_blk)
        o = attend(_dot_split2(q, kv[:, 0:LANES], nt=True), ok, kv[:, LANES:2 * LANES],
                   s_new, new[:, LANES:2 * LANES], has_new)
        o_slc = jnp.where(row // NSA_HPG == g, o, o_slc)

    win = win_ref[0].T
    wpos = past - n_win + lax.broadcasted_iota(I32, (1, n_win), 1)
    wmask = (wpos >= 0) & (wpos > past - NSA_WINDOW)
    sw_new = jnp.sum(q * new[:, 2 * LANES:3 * LANES], axis=-1, keepdims=True)
    o_win = attend(_dot_split2(q, win[:, 0:LANES], nt=True), wmask, win[:, LANES:2 * LANES],
                   sw_new, new[:, 3 * LANES:4 * LANES], True)

    o = gates[:, 0:1] * ocmp_ref[0] + gates[:, 1:2] * o_slc + gates[:, 2:3] * o_win
    o = jnp.where(row // NSA_HPG == 0, o, pltpu.roll(o, NSA_HD, 1))
    o_ref[0] = o[:, 0:NSA_HD]
    win_row = lax.broadcasted_iota(I32, (n_win, 1), 0)
    wout_ref[0] = jnp.where(win_row == n_win - 1, new[:, 2 * LANES:4 * LANES], pltpu.roll(win, n_win - 1, 0))


def _nsa_sample_attend(table, ids, cache, q128, o_cmp, gates, new, win, past):
    batch, n_pages = table.shape
    n_win = win.shape[2]
    assert n_win == NSA_WINDOW and past >= NSA_WINDOW
    per_b = lambda shape: pl.BlockSpec((1,) + shape, lambda b, t, i: (b, 0, 0))
    return pl.pallas_call(
        functools.partial(_nsa_sample_attend_kernel, n_pages=n_pages, past=past),
        grid_spec=pltpu.PrefetchScalarGridSpec(
            num_scalar_prefetch=2, grid=(batch,),
            in_specs=[pl.BlockSpec(memory_space=pl.ANY), per_b((NSA_H, LANES)), per_b((NSA_H, LANES)),
                      per_b((NSA_H, LANES)), per_b((1, 512)), per_b((256, n_win))],
            out_specs=[per_b((NSA_H, NSA_HD)), per_b((n_win, 256))],
            scratch_shapes=[pltpu.VMEM((2, NSA_G * SEL_TOPN, 2 * LANES, PAGE), F32), pltpu.SemaphoreType.DMA((2,))]),
        out_shape=[jax.ShapeDtypeStruct((batch, NSA_H, NSA_HD), F32), jax.ShapeDtypeStruct((batch, n_win, 256), F32)],
        compiler_params=_cparams(1), name="nsa_sample_attend")(
            table.reshape(-1), ids, cache, q128, o_cmp, gates, new, win)


def _ret_sample_kernel(q_ref, k_ref, v_ref, g_ref, cos_ref, sin_ref, gam_ref, s0_ref, o_ref, s_ref):
    cos, sin = cos_ref[...], sin_ref[...]
    q = _rope_rows(q_ref[0], cos, sin)
    k = _rope_rows(k_ref[0], cos, sin) * (RET_DK ** -0.5)
    v = v_ref[0]
    gam = gam_ref[...]
    inner = jnp.sum(q * k, axis=-1, keepdims=True)
    eye = lax.broadcasted_iota(I32, (RET_DK, RET_DK), 0) == lax.broadcasted_iota(I32, (RET_DK, RET_DK), 1)
    qx = q * gam
    rows = []
    for h in range(RET_H):
        s0 = s0_ref[0, h]
        rows.append(_dot_split(qx, *_split_bf16(s0))[h:h + 1])
        k_col = jnp.sum(jnp.where(eye, k[h:h + 1], 0.0), axis=1, keepdims=True)
        s_ref[0, h] = gam[h:h + 1, 0:1] * s0 + k_col * v[h:h + 1]
    o = inner * v + jnp.concatenate(rows, axis=0)
    o_ref[0] = _head_norm_gate(o, g_ref[0])


def _ret_sample(q, k, v, g, s0, past):
    batch = q.shape[0]
    cos, sin = _rope_tables(jnp.full((1,), past), RET_DK // 2)
    gam = jnp.exp(jnp.log1p(-jnp.exp2(-5.0 - jnp.arange(RET_H, dtype=F32))))[:, None] * jnp.ones((1, LANES), F32)
    row = pl.BlockSpec((1, RET_H, LANES), lambda b: (b, 0, 0))
    const = lambda a: pl.BlockSpec(a.shape, lambda b: (0,) * a.ndim)
    state = pl.BlockSpec((1, RET_H, RET_DK, RET_DV), lambda b: (b, 0, 0, 0))
    return pl.pallas_call(
        _ret_sample_kernel, grid=(batch,),
        in_specs=[row, row, row, row, const(cos), const(sin), const(gam), state],
        out_specs=[row, state],
        out_shape=[jax.ShapeDtypeStruct((batch, RET_H, RET_DV), F32), jax.ShapeDtypeStruct(s0.shape, F32)],
        compiler_params=_cparams(1), name="ret_sample")(q, k, v, g, cos, sin, gam, s0)


def _group_lanes(q):
    g = (jnp.arange(NSA_H) // NSA_HPG)[None, :, None]
    z = jnp.zeros_like(q)
    return jnp.concatenate([jnp.where(g == 0, q, z), jnp.where(g == 1, q, z)], axis=-1)


def _mixer_ab_sample(hs, cache, win, s0, table, w1bd, w2bd, pe_l):
    batch = hs.shape[0]
    past = table.shape[1] * PAGE
    cache = jnp.transpose(cache, (0, 2, 3, 4, 1)).reshape(cache.shape[0], 4 * NSA_G * NSA_HD, PAGE)
    win_t = jnp.transpose(win, (0, 2, 3, 4, 1)).reshape(batch, 2 * NSA_G * NSA_HD, win.shape[1])
    q128 = _group_lanes(hs[:, AB_Q:AB_Q + 512].reshape(batch, NSA_H, NSA_HD))
    o_cmp, ids = _nsa_sample_cmp(table, cache, q128, w1bd, w2bd, pe_l, past)
    gates = jnp.pad(hs[:, AB_GATE:AB_GATE + 24].reshape(batch, NSA_H, 3), ((0, 0), (0, 0), (0, LANES - 3)))
    new = hs[:, AB_KV + 256:AB_KV + 768].reshape(batch, 1, 512)
    o_nsa, win_new = _nsa_sample_attend(table, ids[:, 0, :NSA_G * SEL_TOPN].reshape(-1), cache, q128, o_cmp, gates,
                                        new, win_t, past)
    seg = lambda c: hs[:, c:c + 512].reshape(batch, RET_H, RET_DK)
    o_ret, s_new = _ret_sample(seg(AB_QR), seg(AB_KR), seg(AB_VR), seg(AB_GR), s0, past)
    mixed = jnp.concatenate([o_nsa.reshape(batch, 512), o_ret.reshape(batch, 512)], axis=1)
    return mixed, win_new.reshape(win.shape), s_new


def _dev_mixer_ab_sample(x, cache, win, s0, table, w_in, w_out, w_cmp1, w_cmp2, cmp_pe):
    batch = x.shape[0]
    xs = jnp.pad(x.reshape(batch, -1), ((0, TOK_TILE - batch), (0, 0)))
    hs = _mm(xs, *_prep_w_in_ab(w_in), precise_from=0)[:batch]
    mixed, win_new, s_new = _mixer_ab_sample(hs, cache, win, s0, table,
                                             *_prep_compress(w_cmp1, w_cmp2, cmp_pe, precise=True))
    mix = _mm(jnp.pad(mixed, ((0, TOK_TILE - batch), (0, 0))), *_split_bf16(w_out), precise_from=0)[:batch]
    new_kv = hs[:, AB_KV:AB_KV + 512].reshape(batch, 1, 4, NSA_G, NSA_HD)
    return mix.reshape(batch, 1, -1), new_kv, win_new, s_new


MLA_ROW_W = 384
MLA_QK = LANES
MLA_CW = 1024
MLA_KR, MLA_KR_ROT = 640, 768


def _rot_half_cols(w):
    half = w.shape[-1] // 2
    return jnp.concatenate([-w[..., half:], w[..., :half]], axis=-1)


def _prep_mla(w_in, w_qb, w_kb, w_vb):
    d = w_in.shape[0]
    kr = w_in[:, MLA_Q_LORA + MLA_KV_LORA:]
    z = lambda n: jnp.zeros((d, n), w_in.dtype)
    wc = jnp.concatenate([w_in[:, :MLA_Q_LORA + MLA_KV_LORA], kr, z(LANES - MLA_ROPE), _rot_half_cols(kr),
                          z(MLA_CW - MLA_KR_ROT - MLA_ROPE)], axis=1)
    zq = jnp.zeros((MLA_Q_LORA, MLA_H, MLA_QK - MLA_NOPE - MLA_ROPE), w_qb.dtype)
    wq = jnp.concatenate([w_qb, zq], axis=-1).reshape(MLA_Q_LORA, MLA_H * MLA_QK)
    wq_rot = jnp.concatenate([jnp.zeros_like(w_qb[..., :MLA_NOPE]), _rot_half_cols(w_qb[..., MLA_NOPE:]), zq],
                             axis=-1).reshape(MLA_Q_LORA, MLA_H * MLA_QK)
    eye = jnp.eye(MLA_ROPE, dtype=w_kb.dtype)
    wk = jnp.zeros((MLA_ROW_W, MLA_H, MLA_QK), w_kb.dtype)
    wk = wk.at[:MLA_KV_LORA, :, :MLA_NOPE].set(w_kb)
    wk = wk.at[MLA_KV_LORA:MLA_KV_LORA + MLA_ROPE, :, MLA_NOPE:MLA_NOPE + MLA_ROPE].set(
        jnp.broadcast_to(eye[:, None, :], (MLA_ROPE, MLA_H, MLA_ROPE)))
    wv = jnp.zeros((MLA_ROW_W, MLA_H, MLA_V), w_vb.dtype).at[:MLA_KV_LORA].set(w_vb)
    wkv = jnp.concatenate([wk.reshape(MLA_ROW_W, -1), wv.reshape(MLA_ROW_W, -1)], axis=1)
    wabs = jnp.zeros((MLA_H, MLA_QK, MLA_ROW_W), w_kb.dtype)
    wabs = wabs.at[:, :MLA_NOPE, :MLA_KV_LORA].set(jnp.transpose(w_kb, (1, 2, 0)))
    wabs = wabs.at[:, MLA_NOPE:MLA_NOPE + MLA_ROPE, MLA_KV_LORA:MLA_KV_LORA + MLA_ROPE].set(
        jnp.broadcast_to(eye[None], (MLA_H, MLA_ROPE, MLA_ROPE)))
    wvt = jnp.transpose(w_vb, (1, 0, 2))
    return (wc.astype(BF16), wq.astype(BF16), wq_rot.astype(BF16), wkv.astype(BF16), wabs.astype(BF16),
            wvt.astype(BF16))


def _mla_rope_tables(pos):
    half = MLA_ROPE // 2
    inv = ROPE_THETA ** (-jnp.arange(half, dtype=F32) / half)
    ang = pos.astype(F32)[:, None] * inv[None, :]
    cos, sin = jnp.cos(ang), jnp.sin(ang)
    n = pos.shape[0]
    cos2, sin2 = jnp.concatenate([cos, cos], -1), jnp.concatenate([sin, sin], -1)
    z = lambda w: jnp.zeros((n, w), F32)
    cos_k = jnp.concatenate([cos2, z(LANES - MLA_ROPE)], -1)
    sin_k = jnp.concatenate([sin2, z(LANES - MLA_ROPE)], -1)
    cos_q = jnp.concatenate([jnp.ones((n, MLA_NOPE), F32), cos2, z(MLA_QK - MLA_NOPE - MLA_ROPE)], -1)
    sin_q = jnp.concatenate([z(MLA_NOPE), sin2, z(MLA_QK - MLA_NOPE - MLA_ROPE)], -1)
    return cos_k, sin_k, cos_q, sin_q


def _rms_rows(x, g):
    return x * lax.rsqrt(jnp.mean(x * x, -1, keepdims=True) + RMS_EPS) * g


def _mla_in_kernel(x_ref, w_ref, gq_ref, gkv_ref, cos_ref, sin_ref, cq_ref, rows_ref):
    xb = x_ref[...].astype(BF16)
    h = jnp.concatenate([jnp.dot(xb, w_ref[:, c:c + 2 * MXU_N], preferred_element_type=F32)
                         for c in range(0, MLA_CW, 2 * MXU_N)], axis=1)
    cq_ref[...] = _rms_rows(h[:, :MLA_Q_LORA], gq_ref[...]).astype(cq_ref.dtype)
    rows_ref[:, :MLA_KV_LORA] = _rms_rows(h[:, MLA_Q_LORA:MLA_Q_LORA + MLA_KV_LORA], gkv_ref[...])
    rows_ref[:, MLA_KV_LORA:] = (h[:, MLA_KR:MLA_KR + LANES] * cos_ref[...]
                                 + h[:, MLA_KR_ROT:MLA_KR_ROT + LANES] * sin_ref[...])


def _pos_block(n_prompt_tiles, tiles_per_seq):
    return lambda i: (jnp.where(i < n_prompt_tiles, i % tiles_per_seq, tiles_per_seq + i - n_prompt_tiles), 0)


def _mla_in(x, wc, gq, gkv, cos_k, sin_k, n_prompt_tiles, tiles_per_seq, tile=TOK_TILE):
    n, d = x.shape
    pos = pl.BlockSpec((tile, LANES), _pos_block(n_prompt_tiles, tiles_per_seq))
    vec = lambda w: pl.BlockSpec((1, w), lambda i: (0, 0))
    return pl.pallas_call(
        _mla_in_kernel, grid=(n // tile,),
        in_specs=[pl.BlockSpec((tile, d), lambda i: (i, 0)), pl.BlockSpec((d, MLA_CW), lambda i: (0, 0)),
                  vec(MLA_Q_LORA), vec(MLA_KV_LORA), pos, pos],
        out_specs=[pl.BlockSpec((tile, MLA_Q_LORA), lambda i: (i, 0)), pl.BlockSpec((tile, MLA_ROW_W), lambda i: (i, 0))],
        out_shape=[jax.ShapeDtypeStruct((n, MLA_Q_LORA), BF16), jax.ShapeDtypeStruct((n, MLA_ROW_W), F32)],
        compiler_params=_cparams(1), name="mla_in")(
            x, wc, gq.reshape(1, -1), gkv.reshape(1, -1), cos_k, sin_k)


def _mla_q_kernel(c_ref, w_ref, wrot_ref, cos_ref, sin_ref, q_ref):
    c = c_ref[...]
    cos, sin = cos_ref[...], sin_ref[...]
    for h in range(MLA_H):
        cols = slice(h * MLA_QK, (h + 1) * MLA_QK)
        q = jnp.dot(c, w_ref[:, cols], preferred_element_type=F32)
        qr = jnp.dot(c, wrot_ref[:, cols], preferred_element_type=F32)
        q_ref[:, cols] = (q * cos + qr * sin).astype(q_ref.dtype)


def _mla_q(cq, wq, wq_rot, cos_q, sin_q, n_prompt_tiles, tiles_per_seq, tile=TOK_TILE):
    n, k = cq.shape
    m = wq.shape[1]
    pos = pl.BlockSpec((tile, LANES), _pos_block(n_prompt_tiles, tiles_per_seq))
    w = pl.BlockSpec((k, m), lambda i: (0, 0))
    return pl.pallas_call(
        _mla_q_kernel, grid=(n // tile,),
        in_specs=[pl.BlockSpec((tile, k), lambda i: (i, 0)), w, w, pos, pos],
        out_specs=pl.BlockSpec((tile, m), lambda i: (i, 0)),
        out_shape=jax.ShapeDtypeStruct((n, m), BF16),
        compiler_params=_cparams(1), name="mla_q")(cq, wq, wq_rot, cos_q, sin_q)


MLA_ATT_TILE = 512
MLA_ATT_CHUNK = 256


def _mla_prompt_kernel(q_ref, k_ref, v_ref, o_ref):
    ck = MLA_ATT_CHUNK
    n_sub = q_ref.shape[0] // ck
    qi = pl.program_id(2)
    c2 = (MLA_NOPE + MLA_ROPE) ** -0.5 * math.log2(math.e)
    lane = lax.broadcasted_iota(I32, (ck, LANES), 1)
    diag = lax.broadcasted_iota(I32, (ck, ck), 1) <= lax.broadcasted_iota(I32, (ck, ck), 0)
    problems = [(hh, r) for r in range(n_sub) for hh in range(2)]
    q = {(hh, r): q_ref[r * ck:(r + 1) * ck, hh * MLA_QK:(hh + 1) * MLA_QK] for hh, r in problems}

    def step(prob, kc, carry, masked):
        hh, _ = prob
        m, l, acc = carry
        rows = pl.ds(pl.multiple_of(kc * ck, ck), ck)
        s = _dot_nt(q[prob], k_ref[rows, hh * MLA_QK:(hh + 1) * MLA_QK])
        if masked:
            s = jnp.where(diag, s, -jnp.inf)
        m_new = jnp.maximum(m, jnp.max(s, -1, keepdims=True))
        a = jnp.exp2((m - m_new) * c2)
        e = jnp.exp2((s - m_new) * c2)
        l = a * l + jnp.sum(e, -1, keepdims=True)
        acc = a * acc + jnp.dot(e.astype(BF16), v_ref[rows, :], preferred_element_type=F32)
        return m_new, l, acc

    init = tuple((jnp.full((ck, 1), -jnp.inf, F32), jnp.zeros((ck, 1), F32), jnp.zeros((ck, LANES), F32))
                 for _ in problems)

    def full_chunks(kc, carries):
        return tuple(step(p, kc, c, False) for p, c in zip(problems, carries))

    carries = list(lax.fori_loop(0, n_sub * qi, full_chunks, init))
    for idx, prob in enumerate(problems):
        r = prob[1]
        for c in range(r + 1):
            carries[idx] = step(prob, n_sub * qi + c, carries[idx], c == r)
    for r in range(n_sub):
        outs = []
        for hh in range(2):
            m, l, acc = carries[problems.index((hh, r))]
            outs.append(acc * (1.0 / l))
        o_ref[r * ck:(r + 1) * ck, :] = jnp.where(lane < MLA_V, outs[0], outs[1]).astype(o_ref.dtype)


def _mla_prompt(q, kv, batch, seq):
    tq = min(MLA_ATT_TILE, seq)
    assert tq % MLA_ATT_CHUNK == 0
    nq = seq // tq
    k_cols = MLA_H * MLA_QK // (2 * MLA_QK)
    return pl.pallas_call(
        _mla_prompt_kernel, grid=(batch, MLA_H // 2, nq),
        in_specs=[pl.BlockSpec((tq, 2 * MLA_QK), lambda b, j, i: (b * nq + i, j)),
                  pl.BlockSpec((seq, 2 * MLA_QK), lambda b, j, i: (b, j)),
                  pl.BlockSpec((seq, 2 * MLA_V), lambda b, j, i: (b, 2 * k_cols + j))],
        out_specs=pl.BlockSpec((tq, 2 * MLA_V), lambda b, j, i: (b * nq + i, j)),
        out_shape=jax.ShapeDtypeStruct((batch * seq, MLA_H * MLA_V), BF16),
        compiler_params=_cparams(3), name="mla_prompt")(q, kv, kv)


def _mla_absorb_kernel(q_ref, w_ref, o_ref):
    o_ref[0] = jnp.dot(q_ref[...], w_ref[0], preferred_element_type=F32)


def _mla_absorb(qs, wabs):
    batch = qs.shape[0]
    return pl.pallas_call(
        _mla_absorb_kernel, grid=(MLA_H,),
        in_specs=[pl.BlockSpec((batch, MLA_QK), lambda h: (0, h)), pl.BlockSpec((1, MLA_QK, MLA_ROW_W), lambda h: (h, 0, 0))],
        out_specs=pl.BlockSpec((1, batch, MLA_ROW_W), lambda h: (h, 0, 0)),
        out_shape=jax.ShapeDtypeStruct((MLA_H, batch, MLA_ROW_W), F32),
        compiler_params=_cparams(1), name="mla_absorb")(qs, wabs)


MLA_PAGE_GROUP = 8


def _mla_sample_kernel(tbl_ref, cache_ref, q_ref, new_ref, o_ref, buf, pbuf, s_scr, e_scr, sem, *, n_pages):
    b = pl.program_id(0)
    nb = pl.num_programs(0)
    slot = b % 2
    width = MLA_KV_LORA + MLA_ROPE

    def page(bb, p, sl):
        return pltpu.make_async_copy(cache_ref.at[tbl_ref[bb * n_pages + p]], buf.at[sl, p], sem.at[sl])

    def fetch(bb, sl):
        def body(p, c):
            page(bb, p, sl).start()
            return c
        lax.fori_loop(0, n_pages, body, 0)

    @pl.when(b == 0)
    def _():
        fetch(b, slot)

    @pl.when(b + 1 < nb)
    def _():
        fetch(b + 1, 1 - slot)

    def wait(p, c):
        page(b, 0, slot).wait()
        return c
    lax.fori_loop(0, n_pages, wait, 0)

    scale = (MLA_NOPE + MLA_ROPE) ** -0.5
    q = q_ref[0]
    qb = q[:, :width].astype(BF16)

    group = math.gcd(n_pages, MLA_PAGE_GROUP)

    def score(c, carry):
        pages = [buf[slot, c * group + i].astype(BF16) for i in range(group)]
        for i in range(group):
            pbuf[c * group + i] = pages[i]
        s = jnp.dot(qb, jnp.concatenate(pages, axis=1), preferred_element_type=F32)
        for i in range(group):
            s_scr[c * group + i] = s[:, i * PAGE:(i + 1) * PAGE]
        return carry
    lax.fori_loop(0, n_pages // group, score, 0)

    s = s_scr[...] * scale
    new = new_ref[0]
    s_new = jnp.sum(q * new, axis=-1, keepdims=True) * scale
    m = jnp.maximum(jnp.max(jnp.max(s, axis=0), -1, keepdims=True), s_new)
    e, e_new = jnp.exp(s - m), jnp.exp(s_new - m)
    den = jnp.sum(jnp.sum(e, axis=0), -1, keepdims=True) + e_new
    e_scr[...] = e.astype(BF16)

    def weighted(c, acc):
        probs = jnp.concatenate([e_scr[c * group + i] for i in range(group)], axis=1)
        latents = jnp.concatenate([pbuf[c * group + i, 0:MLA_KV_LORA, :] for i in range(group)], axis=1)
        return acc + _dot_nt(probs, latents)
    o = lax.fori_loop(0, n_pages // group, weighted, jnp.zeros((MLA_H, MLA_KV_LORA), F32))
    o_ref[0] = (o + e_new * new[:, :MLA_KV_LORA]) * (1.0 / den)


def _mla_sample(table, cache, qabs, new_rows):
    batch, n_pages = table.shape
    width = cache.shape[1]
    return pl.pallas_call(
        functools.partial(_mla_sample_kernel, n_pages=n_pages),
        grid_spec=pltpu.PrefetchScalarGridSpec(
            num_scalar_prefetch=1, grid=(batch,),
            in_specs=[pl.BlockSpec(memory_space=pl.ANY), pl.BlockSpec((1, MLA_H, MLA_ROW_W), lambda b, t: (b, 0, 0)),
                      pl.BlockSpec((1, 1, MLA_ROW_W), lambda b, t: (b, 0, 0))],
            out_specs=pl.BlockSpec((1, MLA_H, MLA_KV_LORA), lambda b, t: (b, 0, 0)),
            scratch_shapes=[pltpu.VMEM((2, n_pages, width, PAGE), F32), pltpu.VMEM((n_pages, width, PAGE), BF16),
                            pltpu.VMEM((n_pages, MLA_H, PAGE), F32),
                            pltpu.VMEM((n_pages, MLA_H, PAGE), BF16), pltpu.SemaphoreType.DMA((2,))]),
        out_shape=jax.ShapeDtypeStruct((batch, MLA_H, MLA_KV_LORA), F32),
        compiler_params=_cparams(1), name="mla_sample")(table.reshape(-1), cache, qabs, new_rows)


def _mla_vup_kernel(o_ref, w_ref, y_ref):
    y_ref[0] = jnp.dot(o_ref[0].astype(BF16), w_ref[0], preferred_element_type=F32)


def _mla_vup(o_lat, wvt):
    _, batch, lat = o_lat.shape
    return pl.pallas_call(
        _mla_vup_kernel, grid=(MLA_H,),
        in_specs=[pl.BlockSpec((1, batch, lat), lambda h: (h, 0, 0)), pl.BlockSpec((1, lat, MLA_V), lambda h: (h, 0, 0))],
        out_specs=pl.BlockSpec((1, batch, MLA_V), lambda h: (h, 0, 0)),
        out_shape=jax.ShapeDtypeStruct((MLA_H, batch, MLA_V), F32),
        compiler_params=_cparams(1), name="mla_vup")(o_lat, wvt)


def _mixer_c(x, n_prompt, batch_p, seq, batch_s, cache, table, w_in, gq, gkv, w_qb, w_kb, w_vb):
    n = x.shape[0]
    past = table.shape[1] * PAGE
    wc, wq, wq_rot, wkv, wabs, wvt = _prep_mla(w_in, w_qb, w_kb, w_vb)
    pos = jnp.concatenate([jnp.arange(seq), jnp.full((n - n_prompt,), past)])
    cos_k, sin_k, cos_q, sin_q = _mla_rope_tables(pos)
    n_pt, tps = n_prompt // TOK_TILE, seq // TOK_TILE
    cq, rows = _mla_in(x, wc, gq, gkv, cos_k, sin_k, n_pt, tps)
    q = _mla_q(cq, wq, wq_rot, cos_q, sin_q, n_pt, tps)
    kv = _mm(rows, wkv, out_dtype=BF16)
    attn_p = _mla_prompt(q, kv, batch_p, seq)
    qabs = _mla_absorb(q[n_prompt:n_prompt + batch_s], wabs)
    new_rows = rows[n_prompt:n_prompt + batch_s].reshape(batch_s, 1, MLA_ROW_W)
    o_lat = _mla_sample(table, jnp.transpose(cache, (0, 2, 1)), jnp.transpose(qabs, (1, 0, 2)), new_rows)
    o_s = _mla_vup(jnp.transpose(o_lat, (1, 0, 2)), wvt)
    o_s = jnp.transpose(o_s, (1, 0, 2)).reshape(batch_s, MLA_H * MLA_V).astype(BF16)
    attn = jnp.concatenate([attn_p, o_s, jnp.zeros((n - n_prompt - batch_s, MLA_H * MLA_V), BF16)], axis=0)
    return attn, rows


def _dev_mixer_c(xp, xs, cache, table, w_in, gq, gkv, w_qb, w_kb, w_vb, w_out):
    bp, seq, d = xp.shape
    bs = xs.shape[0]
    n_prompt = bp * seq
    x = jnp.concatenate([xp.reshape(n_prompt, d), xs.reshape(bs, d), jnp.zeros((TOK_TILE - bs, d), F32)], axis=0)
    attn, rows = _mixer_c(x, n_prompt, bp, seq, bs, cache, table, w_in, gq, gkv, w_qb, w_kb, w_vb)
    mix = _mm(attn, w_out.astype(BF16))
    width = MLA_KV_LORA + MLA_ROPE
    return ((mix[:n_prompt].reshape(bp, seq, d), rows[:n_prompt, :width].reshape(bp, seq, width)),
            (mix[n_prompt:n_prompt + bs].reshape(bs, 1, d), rows[n_prompt:n_prompt + bs, :width].reshape(bs, 1, width)))


def kernel(x_prompt, x_sample, cache_nsa_kv, state_nsa_win, state_ret, cache_mla, page_table, w_in_ab, w_out_ab, w_cmp1, w_cmp2, cmp_pe, w_in_mla, mla_q_norm, mla_kv_norm, w_q_up, w_k_up, w_v_up, w_out_mla, ln_mix_g, ln_mix_b, ln_ffn_g, ln_ffn_b, w_router, router_bias, w_exp_gate, w_exp_up, w_exp_down):
    bp, seq, d = x_prompt.shape
    bs = x_sample.shape[0]
    assert x_sample.shape[1] == 1 and bs <= TOK_TILE and seq % TOK_TILE == 0 and ln_mix_g.shape[0] == DEPTH
    n_prompt = bp * seq
    pad_rows = TOK_TILE - bs
    x = jnp.concatenate([x_prompt.reshape(n_prompt, d), x_sample.reshape(bs, d), jnp.zeros((pad_rows, d), F32)], axis=0)
    sample = slice(n_prompt, n_prompt + bs)
    sample_tile = n_prompt // TOK_TILE
    mla_w = MLA_KV_LORA + MLA_ROPE
    kv_p, kv_s, win_p, win_s, ret_p, ret_s, mla_p, mla_s = [], [], [], [], [], [], [], []
    for layer in range(DEPTH):
        i = layer // 2
        if layer % 2 == 0:
            h = _mm(x, *_prep_w_in_ab(w_in_ab[i]), precise_from=sample_tile)
            kvc = _compress_prompt(h, bp, seq, *_prep_compress(w_cmp1[i], w_cmp2[i], cmp_pe[i], precise=False))
            o_nsa = _nsa_prompt(h, kvc, bp, seq)
            o_ret, s_p = _ret_prompt(h, bp, seq)
            hs = h[sample]
            mixed_s, w_s, s_s = _mixer_ab_sample(hs, cache_nsa_kv[i], state_nsa_win[i], state_ret[i], page_table,
                                                 *_prep_compress(w_cmp1[i], w_cmp2[i], cmp_pe[i], precise=True))
            mixed = jnp.concatenate([jnp.concatenate([o_nsa, o_ret], axis=1), mixed_s,
                                     jnp.zeros((pad_rows, mixed_s.shape[1]), F32)], axis=0)
            w_out, w_out_lo = _split_bf16(w_out_ab[i])
            out_precise = sample_tile
            kv_p.append(h[:n_prompt, AB_KV:AB_KV + 512].reshape(bp, seq, 4, NSA_G, NSA_HD))
            kv_s.append(hs[:, AB_KV:AB_KV + 512].reshape(bs, 1, 4, NSA_G, NSA_HD))
            n_keep = min(NSA_WINDOW, seq)
            win_p.append(h[:n_prompt, AB_KV + 512:AB_KV + 768].reshape(bp, seq, 2, NSA_G, NSA_HD)[:, seq - n_keep:])
            win_s.append(w_s)
            ret_p.append(s_p)
            ret_s.append(s_s)
        else:
            mixed, rows = _mixer_c(x, n_prompt, bp, seq, bs, cache_mla[i], page_table, w_in_mla[i], mla_q_norm[i],
                                   mla_kv_norm[i], w_q_up[i], w_k_up[i], w_v_up[i])
            w_out, w_out_lo, out_precise = w_out_mla[i].astype(BF16), None, None
            mla_p.append(rows[:n_prompt, :mla_w].reshape(bp, seq, mla_w))
            mla_s.append(rows[sample, :mla_w].reshape(bs, 1, mla_w))
        x = _proj_ln(mixed, w_out, x, ln_mix_g[layer], ln_mix_b[layer], w_lo=w_out_lo, precise_from=out_precise)
        x = _moe_ln(x, w_router, router_bias, w_exp_gate[layer].astype(BF16), w_exp_up[layer].astype(BF16),
                    w_exp_down[layer].astype(BF16), ln_ffn_g[layer], ln_ffn_b[layer])
    return (x[:n_prompt].reshape(bp, seq, d), x[sample].reshape(bs, 1, d), jnp.stack(kv_p), jnp.stack(kv_s),
            jnp.stack(win_p), jnp.stack(win_s), jnp.stack(ret_p), jnp.stack(ret_s), jnp.stack(mla_p), jnp.stack(mla_s))
```

```python
import functools
import math

import numpy as np
import jax
import jax.numpy as jnp
from jax import lax
from jax.experimental import pallas as pl
from jax.experimental.pallas import tpu as pltpu

F32 = jnp.float32
BF16 = jnp.bfloat16
I32 = jnp.int32

PAGE = 128
NSA_H, NSA_G, NSA_HD = 8, 2, 64
NSA_HPG = NSA_H // NSA_G
CMP_BLOCK, CMP_STRIDE, CMP_HIDDEN = 32, 16, 64
SEL_BLOCK, SEL_TOPN, SEL_LOCAL = 64, 8, 2
NSA_WINDOW = 512
FORCE_SCORE = 1.0e6
RET_H, RET_DK, RET_DV, RET_CHUNK = 4, 128, 128, 128
MLA_H, MLA_Q_LORA, MLA_KV_LORA, MLA_NOPE, MLA_ROPE, MLA_V = 16, 384, 256, 64, 32, 64
ROPE_THETA = 10000.0
N_EXPERTS, N_GROUPS, EPG, D_EXPERT = 16, 4, 4, 512
N_PAIRS = EPG * (EPG - 1) // 2
N_CLASSES = N_GROUPS * N_PAIRS
Q_BLOCK = 128
LN_EPS = 1e-5
RMS_EPS = 1e-6
DEPTH = 2
ALPHA = (2 * DEPTH) ** 0.25

LANES = 128
MXU_N = 256
VMEM_LIMIT = 48 * 1024 * 1024

TOK_TILE = 512
MOE_TILE = 256
CLS_PAD = 32

AB_Q, AB_QR, AB_KR, AB_VR, AB_GR, AB_KV, AB_GATE = 0, 512, 1024, 1536, 2048, 2560, 3328
AB_W = 3584


def _cparams(n_axes):
    return pltpu.CompilerParams(dimension_semantics=("arbitrary",) * n_axes, vmem_limit_bytes=VMEM_LIMIT)


def _split_bf16(w):
    hi = lax.bitcast_convert_type(lax.bitcast_convert_type(w, jnp.uint32) & jnp.uint32(0xFFFF0000), F32)
    return hi.astype(BF16), (w - hi).astype(BF16)


def _dot_split(x, w_hi, w_lo):
    x_hi, x_lo = _split_bf16(x)
    dot = functools.partial(jnp.dot, preferred_element_type=F32)
    return dot(x_hi, w_hi) + (dot(x_lo, w_hi) + dot(x_hi, w_lo))


def _mm_kernel(x_ref, w_ref, *rest, precise_from):
    o_ref = rest[-1]
    step = 2 * MXU_N

    def single_pass():
        xb = x_ref[...].astype(BF16)
        for c in range(0, o_ref.shape[1], step):
            o_ref[:, c:c + step] = jnp.dot(xb, w_ref[:, c:c + step], preferred_element_type=F32).astype(o_ref.dtype)

    if precise_from is None:
        single_pass()
        return
    pl.when(pl.program_id(0) < precise_from)(single_pass)

    @pl.when(pl.program_id(0) >= precise_from)
    def _():
        x = x_ref[...]
        for c in range(0, o_ref.shape[1], step):
            o_ref[:, c:c + step] = _dot_split(x, w_ref[:, c:c + step], rest[0][:, c:c + step]).astype(o_ref.dtype)


def _mm(x, w, w_lo=None, precise_from=None, out_dtype=F32, tile=TOK_TILE):
    n, k = x.shape
    m = w.shape[1]
    assert n % tile == 0 and m % (2 * MXU_N) == 0
    weights = (w,) if precise_from is None else (w, w_lo)
    return pl.pallas_call(
        functools.partial(_mm_kernel, precise_from=precise_from), grid=(n // tile,),
        in_specs=[pl.BlockSpec((tile, k), lambda i: (i, 0))] + [pl.BlockSpec((k, m), lambda i: (0, 0))] * len(weights),
        out_specs=pl.BlockSpec((tile, m), lambda i: (i, 0)),
        out_shape=jax.ShapeDtypeStruct((n, m), out_dtype),
        compiler_params=_cparams(1), name="mm")(x, *weights)


def _layer_norm_rows(z, g, b):
    mu = jnp.mean(z, -1, keepdims=True)
    zc = z - mu
    var = jnp.mean(zc * zc, -1, keepdims=True)
    return zc * lax.rsqrt(var + LN_EPS) * g + b


def _proj_ln_kernel(a_ref, w_ref, *rest, precise_from):
    x_ref, g_ref, b_ref, o_ref = rest[-4:]

    def finish(y):
        o_ref[...] = _layer_norm_rows(ALPHA * x_ref[...] + y, g_ref[...], b_ref[...])

    def single_pass():
        finish(jnp.dot(a_ref[...].astype(BF16), w_ref[...], preferred_element_type=F32))

    if precise_from is None:
        single_pass()
        return
    pl.when(pl.program_id(0) < precise_from)(single_pass)

    @pl.when(pl.program_id(0) >= precise_from)
    def _():
        finish(_dot_split(a_ref[...], w_ref[...], rest[0][...]))


def _proj_ln(a, w, x, g, b, w_lo=None, precise_from=None, tile=TOK_TILE):
    n, k = a.shape
    d = w.shape[1]
    weights = (w,) if precise_from is None else (w, w_lo)
    row = pl.BlockSpec((1, d), lambda i: (0, 0))
    return pl.pallas_call(
        functools.partial(_proj_ln_kernel, precise_from=precise_from), grid=(n // tile,),
        in_specs=[pl.BlockSpec((tile, k), lambda i: (i, 0))] + [pl.BlockSpec((k, d), lambda i: (0, 0))] * len(weights)
        + [pl.BlockSpec((tile, d), lambda i: (i, 0)), row, row],
        out_specs=pl.BlockSpec((tile, d), lambda i: (i, 0)),
        out_shape=jax.ShapeDtypeStruct((n, d), F32),
        compiler_params=_cparams(1), name="proj_ln")(a, *weights, x, g.reshape(1, d), b.reshape(1, d))


def _router_kernel(x_ref, wrt_ref, wrt_lo_ref, bias_ref, tri_ref, cls_ref, rank_ref, cnt_ref, carry_ref):
    @pl.when(pl.program_id(0) == 0)
    def _():
        carry_ref[...] = jnp.zeros_like(carry_ref)

    tile = x_ref.shape[0]
    x_hi, x_lo = _split_bf16(x_ref[...])
    logits = _dot_nt(wrt_ref[...], x_hi) + (_dot_nt(wrt_ref[...], x_lo) + _dot_nt(wrt_lo_ref[...], x_hi))
    ssel = jax.nn.sigmoid(logits) + bias_ref[...]
    rows = [ssel[e:e + 1, :] for e in range(N_EXPERTS)]

    def top2sum(a, b, c, d):
        return jnp.maximum(jnp.maximum(jnp.maximum(a + b, a + c), jnp.maximum(a + d, b + c)),
                           jnp.maximum(b + d, c + d))

    gscore = [top2sum(*rows[EPG * g:EPG * (g + 1)]) for g in range(N_GROUPS)]
    best, gi = gscore[0], jnp.zeros((1, tile), I32)
    for g in range(1, N_GROUPS):
        better = gscore[g] > best
        gi = jnp.where(better, g, gi)
        best = jnp.where(better, gscore[g], best)
    v = []
    for j in range(EPG):
        vj = rows[j]
        for g in range(1, N_GROUPS):
            vj = jnp.where(gi == g, rows[EPG * g + j], vj)
        v.append(vj)
    sel = []
    for i in range(EPG):
        r = jnp.zeros((1, tile), I32)
        for j in range(EPG):
            if j == i:
                continue
            beats = (v[j] > v[i]) | ((v[j] == v[i]) if j < i else False)
            r = r + beats.astype(I32)
        sel.append(r < 2)
    lo = jnp.where(sel[0], 0, jnp.where(sel[1], 1, 2))
    hi = jnp.where(sel[3], 3, jnp.where(sel[2], 2, 1))
    base = jnp.where(lo == 0, 0, jnp.where(lo == 1, 3, 5))
    cls = gi * N_PAIRS + base + hi - lo - 1

    onehot = (lax.broadcasted_iota(I32, (CLS_PAD, tile), 0) == cls).astype(F32)
    prefix = jnp.dot(onehot.astype(BF16), tri_ref[...], preferred_element_type=F32)
    carry = carry_ref[...]
    rank = jnp.sum(onehot * (prefix - 1.0 + carry), axis=0, keepdims=True)
    carry = carry + jnp.sum(onehot, axis=1, keepdims=True)
    carry_ref[...] = carry
    cls_ref[...] = cls
    rank_ref[...] = rank.astype(I32)
    cnt_ref[...] = carry.astype(I32)


def _route(x, wrt, wrt_lo, bias, tile=TOK_TILE):
    n, d = x.shape
    tri = (np.arange(tile)[:, None] <= np.arange(tile)[None, :]).astype(np.float32)
    w_spec = pl.BlockSpec((N_EXPERTS, d), lambda i: (0, 0))
    return pl.pallas_call(
        _router_kernel, grid=(n // tile,),
        in_specs=[pl.BlockSpec((tile, d), lambda i: (i, 0)), w_spec, w_spec,
                  pl.BlockSpec((N_EXPERTS, 1), lambda i: (0, 0)), pl.BlockSpec((tile, tile), lambda i: (0, 0))],
        out_specs=[pl.BlockSpec((1, tile), lambda i: (0, i)), pl.BlockSpec((1, tile), lambda i: (0, i)),
                   pl.BlockSpec((CLS_PAD, 1), lambda i: (0, 0))],
        out_shape=[jax.ShapeDtypeStruct((1, n), I32), jax.ShapeDtypeStruct((1, n), I32),
                   jax.ShapeDtypeStruct((CLS_PAD, 1), I32)],
        scratch_shapes=[pltpu.VMEM((CLS_PAD, 1), F32)],
        compiler_params=_cparams(1), name="moe_route")(x, wrt, wrt_lo, bias, jnp.asarray(tri, BF16))


def _row_copy(src_ref, src_row, dst_ref, dst_row, sem):
    return pltpu.make_async_copy(src_ref.at[pl.ds(src_row, 1)], dst_ref.at[pl.ds(dst_row, 1)], sem)


ROW_DMA_UNROLL = 8


def _slot_kernel(off_ref, cls_ref, rank_ref, slot_ref):
    cls = cls_ref[...]
    slot = rank_ref[...]
    for c in range(N_CLASSES):
        slot = slot + jnp.where(cls == c, off_ref[c], 0)
    slot_ref[...] = slot


def _slots(cls, rank, off):
    row = pl.BlockSpec(cls.shape, lambda i, off: (0, 0))
    return pl.pallas_call(
        _slot_kernel,
        grid_spec=pltpu.PrefetchScalarGridSpec(num_scalar_prefetch=1, grid=(1,), in_specs=[row, row], out_specs=row),
        out_shape=jax.ShapeDtypeStruct(cls.shape, I32),
        compiler_params=_cparams(1), name="moe_slots")(off, cls, rank)


def _scatter_kernel(slot_ref, x_ref, xs_in_ref, xs_ref, sem):
    del xs_in_ref
    tile = x_ref.shape[0]

    def start(r, c):
        _row_copy(x_ref, r, xs_ref, slot_ref[0, r], sem).start()
        return c

    lax.fori_loop(0, tile, start, 0, unroll=ROW_DMA_UNROLL)

    def wait(r, c):
        _row_copy(x_ref, 0, xs_ref, 0, sem).wait()
        return c

    lax.fori_loop(0, tile, wait, 0, unroll=ROW_DMA_UNROLL)


def _scatter_rows(x, slot, n_slots, tile=MOE_TILE):
    n, d = x.shape
    return pl.pallas_call(
        _scatter_kernel, grid=(n // tile,),
        in_specs=[pl.BlockSpec((1, tile), lambda i: (0, i), memory_space=pltpu.SMEM),
                  pl.BlockSpec((tile, d), lambda i: (i, 0)), pl.BlockSpec(memory_space=pl.ANY)],
        out_specs=pl.BlockSpec(memory_space=pl.ANY),
        scratch_shapes=[pltpu.SemaphoreType.DMA(())],
        out_shape=jax.ShapeDtypeStruct((n_slots, d), F32),
        input_output_aliases={2: 0},
        compiler_params=_cparams(1), name="moe_scatter")(slot, x, jnp.zeros((n_slots, d), F32))


def _expert_kernel(e1_ref, e2_ref, valid_ref, xs_ref, wr_ref, g1_ref, u1_ref, d1_ref, g2_ref, u2_ref, d2_ref,
                   o_ref):
    j = pl.program_id(0)

    @pl.when(valid_ref[j] == 0)
    def _():
        o_ref[...] = jnp.zeros_like(o_ref)

    @pl.when(valid_ref[j] != 0)
    def _():
        xb = xs_ref[...].astype(BF16)
        s = jax.nn.sigmoid(jnp.dot(xb, wr_ref[...], preferred_element_type=F32))
        lane = lax.broadcasted_iota(I32, s.shape, 1)
        w1 = jnp.sum(jnp.where(lane == e1_ref[j], s, 0.0), axis=1, keepdims=True)
        w2 = jnp.sum(jnp.where(lane == e2_ref[j], s, 0.0), axis=1, keepdims=True)
        tot = w1 + w2

        def mlp(g_ref, u_ref, d_ref):
            hg = jnp.dot(xb, g_ref[0], preferred_element_type=F32)
            hu = jnp.dot(xb, u_ref[0], preferred_element_type=F32)
            hdn = hg * jax.nn.sigmoid(hg) * hu
            return jnp.dot(hdn.astype(BF16), d_ref[0], preferred_element_type=F32)

        o_ref[...] = (w1 / tot) * mlp(g1_ref, u1_ref, d1_ref) + (w2 / tot) * mlp(g2_ref, u2_ref, d2_ref)


def _expert_pairs(xs, wr, wg, wu, wd, e1, e2, valid, tile=MOE_TILE):
    n_slots, d = xs.shape
    de = wg.shape[2]
    up = lambda sel: pl.BlockSpec((1, d, de), lambda j, e1, e2, v: ((e1, e2)[sel][j], 0, 0))
    down = lambda sel: pl.BlockSpec((1, de, d), lambda j, e1, e2, v: ((e1, e2)[sel][j], 0, 0))
    return pl.pallas_call(
        _expert_kernel,
        grid_spec=pltpu.PrefetchScalarGridSpec(
            num_scalar_prefetch=3, grid=(n_slots // tile,),
            in_specs=[pl.BlockSpec((tile, d), lambda j, e1, e2, v: (j, 0)),
                      pl.BlockSpec((d, N_EXPERTS), lambda j, e1, e2, v: (0, 0)),
                      up(0), up(0), down(0), up(1), up(1), down(1)],
            out_specs=pl.BlockSpec((tile, d), lambda j, e1, e2, v: (j, 0))),
        out_shape=jax.ShapeDtypeStruct((n_slots, d), F32),
        compiler_params=_cparams(1), name="moe_experts")(e1, e2, valid, xs, wr, wg, wu, wd, wg, wu, wd)


def _gather_ln_kernel(slot_ref, next_ref, x_ref, ys_ref, g_ref, b_ref, o_ref, ybuf, sem):
    tile = x_ref.shape[0]
    i = pl.program_id(0)
    cur = i % 2

    def fetch(idx_ref, buf_slot):
        def start(r, c):
            _row_copy(ys_ref, idx_ref[0, r], ybuf.at[buf_slot], r, sem.at[buf_slot]).start()
            return c
        lax.fori_loop(0, tile, start, 0, unroll=ROW_DMA_UNROLL)

    @pl.when(i == 0)
    def _():
        fetch(slot_ref, cur)

    @pl.when(i + 1 < pl.num_programs(0))
    def _():
        fetch(next_ref, 1 - cur)

    def wait(r, c):
        _row_copy(ys_ref, 0, ybuf.at[cur], 0, sem.at[cur]).wait()
        return c

    lax.fori_loop(0, tile, wait, 0, unroll=ROW_DMA_UNROLL)
    o_ref[...] = _layer_norm_rows(ALPHA * x_ref[...] + ybuf[cur], g_ref[...], b_ref[...])


def _gather_ln(x, ys, slot, g, b, tile=MOE_TILE):
    n, d = x.shape
    n_tiles = n // tile
    vec = pl.BlockSpec((1, d), lambda i: (0, 0))
    return pl.pallas_call(
        _gather_ln_kernel, grid=(n_tiles,),
        in_specs=[pl.BlockSpec((1, tile), lambda i: (0, i), memory_space=pltpu.SMEM),
                  pl.BlockSpec((1, tile), lambda i: (0, jnp.minimum(i + 1, n_tiles - 1)), memory_space=pltpu.SMEM),
                  pl.BlockSpec((tile, d), lambda i: (i, 0)), pl.BlockSpec(memory_space=pl.ANY), vec, vec],
        out_specs=pl.BlockSpec((tile, d), lambda i: (i, 0)),
        scratch_shapes=[pltpu.VMEM((2, tile, d), F32), pltpu.SemaphoreType.DMA((2,))],
        out_shape=jax.ShapeDtypeStruct((n, d), F32),
        compiler_params=_cparams(1), name="moe_gather_ln")(slot, slot, x, ys, g.reshape(1, d), b.reshape(1, d))


_PAIR_LO = np.array([0, 0, 0, 1, 1, 2], np.int32)
_PAIR_HI = np.array([1, 2, 3, 2, 3, 3], np.int32)


def _moe_ln(x, w_router, router_bias, wg, wu, wd, g, b):
    n, d = x.shape
    cls, rank, cnt = _route(x, *_split_bf16(w_router.T), router_bias.reshape(N_EXPERTS, 1).astype(F32))
    cnt = cnt[:N_CLASSES, 0]
    tiles = (cnt + MOE_TILE - 1) // MOE_TILE
    tile_end = jnp.cumsum(tiles)
    off = jnp.zeros((CLS_PAD,), I32).at[:N_CLASSES].set((tile_end - tiles) * MOE_TILE)
    n_tiles = n // MOE_TILE + N_CLASSES
    tile_id = jnp.arange(n_tiles, dtype=I32)
    tile_cls = jnp.minimum(jnp.sum(tile_id[:, None] >= tile_end[None, :], axis=1), N_CLASSES - 1).astype(I32)
    valid = (tile_id < tile_end[-1]).astype(I32)
    grp, pair = tile_cls // N_PAIRS, tile_cls % N_PAIRS
    e1 = grp * EPG + jnp.asarray(_PAIR_LO)[pair]
    e2 = grp * EPG + jnp.asarray(_PAIR_HI)[pair]
    slot = _slots(cls, rank, off)
    xs = _scatter_rows(x, slot, n_tiles * MOE_TILE)
    ys = _expert_pairs(xs, w_router.astype(BF16), wg, wu, wd, e1, e2, valid)
    return _gather_ln(x, ys, slot, g, b)


def _prep_w_in_ab(w):
    k = w.shape[0]
    q, kv, gate, qr, kr, vr, gr = jnp.split(w, np.cumsum([512, 768, 24, 512, 512, 512]).tolist(), axis=1)
    pad = jnp.zeros((k, AB_W - AB_GATE - 24), w.dtype)
    return _split_bf16(jnp.concatenate([q, qr, kr, vr, gr, kv, gate, pad], axis=1))


CMP_GROUP = 4


def _prep_compress(w1, w2, pe, precise):
    hd = NSA_HD
    w1r = w1.reshape(2, 2, CMP_STRIDE, hd, CMP_HIDDEN)
    w1bd = jnp.zeros((2, 2, CMP_STRIDE, NSA_G * hd, NSA_G * CMP_HIDDEN), F32)
    w2bd = jnp.zeros((2, NSA_G * CMP_HIDDEN, NSA_G * hd), F32)
    for g in range(NSA_G):
        w1bd = w1bd.at[..., g * hd:(g + 1) * hd, g * CMP_HIDDEN:(g + 1) * CMP_HIDDEN].set(w1r)
        w2bd = w2bd.at[:, g * CMP_HIDDEN:(g + 1) * CMP_HIDDEN, g * hd:(g + 1) * hd].set(w2)
    w1cat = jnp.concatenate([w1bd[:, 0], w1bd[:, 1]], axis=-1)
    if precise:
        stack = lambda w: jnp.concatenate([_split_bf16(w)[0]] * 2 + [_split_bf16(w)[1]], axis=-2)
    else:
        stack = lambda w: w.astype(BF16)
    w1s = stack(w1cat)
    w1s = w1s.reshape(2, CMP_STRIDE // CMP_GROUP, CMP_GROUP * w1s.shape[2], w1s.shape[3])
    per = pe.reshape(2, 2, CMP_STRIDE, hd)
    pe_l = jnp.concatenate([per] * NSA_G, axis=-1)
    return w1s, stack(w2bd), pe_l


def _split_lhs(x, precise):
    if not precise:
        return x.astype(BF16)
    hi, lo = _split_bf16(x)
    return jnp.concatenate([hi, lo, hi], axis=1)


def _compress_rows(x_ref, kind, n_chunks, pe_ref, w1_ref, w2_ref, precise=False):
    width = NSA_G * CMP_HIDDEN
    pe_pad = 16
    acc = jnp.zeros((n_chunks + 2 * pe_pad, 2 * width), F32)
    for grp in range(CMP_STRIDE // CMP_GROUP):
        blocks = []
        for s in range(grp * CMP_GROUP, (grp + 1) * CMP_GROUP):
            xs = x_ref[pl.ds(s, n_chunks, stride=CMP_STRIDE), :]
            pe_rows = [jnp.broadcast_to(pe_ref[kind, j, s:s + 1, :], (pe_pad, xs.shape[1])) for j in range(2)]
            blocks.append(jnp.concatenate([_split_lhs(xs, precise)] + [_split_lhs(r, precise) for r in pe_rows],
                                          axis=0))
        acc = acc + jnp.dot(jnp.concatenate(blocks, axis=1), w1_ref[kind, grp], preferred_element_type=F32)
    first = acc[0:n_chunks, 0:width] + acc[n_chunks:n_chunks + 1, 0:width]
    second = acc[0:n_chunks, width:] + acc[n_chunks + pe_pad:n_chunks + pe_pad + 1, width:]
    hdn = jax.nn.gelu(first + pltpu.roll(second, n_chunks - 1, 0))
    return jnp.dot(_split_lhs(hdn, precise), w2_ref[kind], preferred_element_type=F32)


def _dot_split2(a, b, nt=False):
    (a_hi, a_lo), (b_hi, b_lo) = _split_bf16(a), _split_bf16(b)
    dot = _dot_nt if nt else functools.partial(jnp.dot, preferred_element_type=F32)
    return dot(a_hi, b_hi) + (dot(a_lo, b_hi) + dot(a_hi, b_lo))


def _compress_kernel(k_ref, v_ref, pe_ref, w1_ref, w2_ref, o_ref):
    n_chunks = o_ref.shape[1]
    o_ref[0, :, 0:LANES] = _compress_rows(k_ref, 0, n_chunks, pe_ref, w1_ref, w2_ref)
    o_ref[0, :, LANES:2 * LANES] = _compress_rows(v_ref, 1, n_chunks, pe_ref, w1_ref, w2_ref)


def _compress_prompt(h, batch, seq, w1bd, w2bd, pe_l):
    n_chunks = seq // CMP_STRIDE
    full = lambda a: pl.BlockSpec(a.shape, lambda b: (0,) * a.ndim)
    return pl.pallas_call(
        _compress_kernel, grid=(batch,),
        in_specs=[pl.BlockSpec((seq, LANES), lambda b: (b, AB_KV // LANES)),
                  pl.BlockSpec((seq, LANES), lambda b: (b, AB_KV // LANES + 1)), full(pe_l), full(w1bd), full(w2bd)],
        out_specs=pl.BlockSpec((1, n_chunks, 256), lambda b: (b, 0, 0)),
        out_shape=jax.ShapeDtypeStruct((batch, n_chunks, 256), F32),
        compiler_params=_cparams(1), name="nsa_compress")(h, h, pe_l, w1bd, w2bd)


def _masked_softmax(s, mask):
    s = jnp.where(mask, s, -jnp.inf)
    m = jnp.max(s, -1, keepdims=True)
    m = jnp.where(m == -jnp.inf, 0.0, m)
    e = jnp.exp(s - m)
    den = jnp.maximum(jnp.sum(e, -1, keepdims=True), jnp.finfo(F32).tiny)
    return e * (1.0 / den)


def _dot_nt(a, b):
    return lax.dot_general(a, b, (((1,), (1,)), ((), ())), preferred_element_type=F32)


def _topn_rows(imp_t, n_top):
    n_blocks = imp_t.shape[0]
    blk = lax.broadcasted_iota(I32, imp_t.shape, 0)
    rank = jnp.zeros(imp_t.shape, I32)
    for j in range(n_blocks):
        row = imp_t[j:j + 1, :]
        rank = rank + ((row > imp_t) | ((row == imp_t) & (blk > j))).astype(I32)
    return rank < n_top


NSA_SEL_CHUNK = 256


def _nsa_prompt_kernel(q_ref, gate_ref, slc_ref, win_ref, kvc_ref, selmap_ref, o_ref, *, seq, win_span):
    qb = q_ref.shape[0]
    i0 = pl.program_id(1) * qb
    n_cmp_pad = kvc_ref.shape[1]
    n_slc = seq // SEL_BLOCK
    ck = min(NSA_SEL_CHUNK, seq)
    scale = NSA_HD ** -0.5
    qpos = i0 + lax.broadcasted_iota(I32, (qb, 1), 0)
    lane = lax.broadcasted_iota(I32, (qb, LANES), 1)
    gates = jax.nn.sigmoid(gate_ref[:, 0:LANES])

    kvc = kvc_ref[0].astype(BF16)
    cmp_last = lax.broadcasted_iota(I32, (1, n_cmp_pad), 1) * CMP_STRIDE + (CMP_BLOCK - 1)
    cmp_mask = cmp_last <= qpos
    w0 = jnp.maximum(i0 + qb - win_span, 0)
    w0 = pl.multiple_of(w0, qb)
    wpos = w0 + lax.broadcasted_iota(I32, (1, win_span), 1)
    wmask = (wpos <= qpos) & (wpos > qpos - NSA_WINDOW)
    k_win = win_ref[pl.ds(w0, win_span), 0:LANES].astype(BF16)
    v_win = win_ref[pl.ds(w0, win_span), LANES:2 * LANES].astype(BF16)
    blk_t = lax.broadcasted_iota(I32, (n_slc, qb), 0)
    qpos_t = i0 + lax.broadcasted_iota(I32, (n_slc, qb), 1)
    cur_t = qpos_t // SEL_BLOCK
    forced_t = (blk_t == 0) | ((blk_t <= cur_t) & (blk_t > cur_t - SEL_LOCAL))
    valid_t = blk_t * SEL_BLOCK <= qpos_t
    chunk_blk = lax.broadcasted_iota(I32, (n_slc, ck), 1) // SEL_BLOCK
    chunk_row = lax.broadcasted_iota(I32, (n_slc, ck), 0)
    chunk_pos = lax.broadcasted_iota(I32, (1, ck), 1)
    n_chunks = (i0 + qb - 1) // ck + 1

    out_tiles = [None] * (NSA_H // 2)
    for g in range(NSA_G):
        heads = list(range(g * NSA_HPG, (g + 1) * NSA_HPG))
        q128, p_cmp = {}, {}
        imp_t = jnp.zeros((n_slc, qb), F32)
        for h in heads:
            t = q_ref[:, (h // 2) * LANES:(h // 2 + 1) * LANES] * scale
            t = jnp.where((lane // NSA_HD) == (h % 2), t, 0.0)
            if h % 2 != g:
                t = pltpu.roll(t, NSA_HD, 1)
            q128[h] = t.astype(BF16)
            p = _masked_softmax(_dot_nt(q128[h], kvc[:, 0:LANES]), cmp_mask)
            p_cmp[h] = p.astype(BF16)
            imp_t = imp_t + _dot_nt(selmap_ref[...], p_cmp[h])
        imp_t = jnp.where(forced_t, FORCE_SCORE, imp_t)
        imp_t = jnp.where(valid_t, imp_t, -jnp.inf)
        sel_t = _topn_rows(imp_t, min(SEL_TOPN, n_slc)).astype(BF16)

        def sel_step(kc, carry):
            rows = pl.ds(pl.multiple_of(kc * ck, ck), ck)
            k = slc_ref[rows, 0:LANES].astype(BF16)
            v = slc_ref[rows, LANES:2 * LANES].astype(BF16)
            in_chunk = (chunk_row == kc * (ck // SEL_BLOCK) + chunk_blk).astype(BF16)
            picked = lax.dot_general(sel_t, in_chunk, (((0,), (0,)), ((), ())), preferred_element_type=F32)
            kmask = (picked > 0.5) & (kc * ck + chunk_pos <= qpos)
            new = []
            for h, (m, l, acc) in zip(heads, carry):
                s = jnp.where(kmask, _dot_nt(q128[h], k), -jnp.inf)
                m_new = jnp.maximum(m, jnp.max(s, -1, keepdims=True))
                a = jnp.exp(m - m_new)
                e = jnp.exp(s - m_new)
                new.append((m_new, a * l + jnp.sum(e, -1, keepdims=True),
                            a * acc + jnp.dot(e.astype(BF16), v, preferred_element_type=F32)))
            return tuple(new)

        init = tuple((jnp.full((qb, 1), -jnp.inf, F32), jnp.zeros((qb, 1), F32), jnp.zeros((qb, LANES), F32))
                     for _ in heads)
        sel_state = lax.fori_loop(0, n_chunks, sel_step, init)
        for h, (m, l, acc) in zip(heads, sel_state):
            o_cmp = jnp.dot(p_cmp[h], kvc[:, LANES:2 * LANES], preferred_element_type=F32)
            o_slc = acc * (1.0 / l)
            p = _masked_softmax(_dot_nt(q128[h], k_win), wmask)
            o_win = jnp.dot(p.astype(BF16), v_win, preferred_element_type=F32)
            o = (gates[:, 3 * h:3 * h + 1] * o_cmp + gates[:, 3 * h + 1:3 * h + 2] * o_slc
                 + gates[:, 3 * h + 2:3 * h + 3] * o_win)
            if h % 2 != g:
                o = pltpu.roll(o, NSA_HD, 1)
            keep = (lane // NSA_HD) == (h % 2)
            prev = out_tiles[h // 2]
            out_tiles[h // 2] = jnp.where(keep, o, 0.0 if prev is None else prev)
    for t, tile in enumerate(out_tiles):
        o_ref[:, t * LANES:(t + 1) * LANES] = tile


def _nsa_prompt(h, kvc, batch, seq):
    nqb = seq // Q_BLOCK
    n_cmp_pad = seq // CMP_STRIDE
    n_slc = seq // SEL_BLOCK
    assert seq % max(Q_BLOCK, min(NSA_SEL_CHUNK, seq)) == 0 and n_slc % 8 == 0 and n_slc >= SEL_LOCAL + 1
    win_span = min(NSA_WINDOW + Q_BLOCK, seq)
    sj = np.arange(n_slc)[:, None] * SEL_BLOCK
    ci = np.arange(n_cmp_pad)[None, :] * CMP_STRIDE
    selmap = np.clip(np.minimum(ci + CMP_BLOCK, sj + SEL_BLOCK) - np.maximum(ci, sj), 0, None) / CMP_STRIDE
    selmap[:, n_cmp_pad - 1:] = 0.0
    const = lambda a: pl.BlockSpec(a.shape, lambda b, i: (0,) * a.ndim)
    selmap = jnp.asarray(selmap, BF16)
    return pl.pallas_call(
        functools.partial(_nsa_prompt_kernel, seq=seq, win_span=win_span), grid=(batch, nqb),
        in_specs=[pl.BlockSpec((Q_BLOCK, 512), lambda b, i: (b * nqb + i, AB_Q // 512)),
                  pl.BlockSpec((Q_BLOCK, 256), lambda b, i: (b * nqb + i, AB_GATE // 256)),
                  pl.BlockSpec((seq, 256), lambda b, i: (b, (AB_KV + 256) // 256)),
                  pl.BlockSpec((seq, 256), lambda b, i: (b, (AB_KV + 512) // 256)),
                  pl.BlockSpec((1, n_cmp_pad, 256), lambda b, i: (b, 0, 0)),
                  const(selmap)],
        out_specs=pl.BlockSpec((Q_BLOCK, 512), lambda b, i: (b * nqb + i, 0)),
        out_shape=jax.ShapeDtypeStruct((batch * seq, 512), F32),
        compiler_params=_cparams(2), name="nsa_prompt")(h, h, h, h, kvc, selmap)


def _rope_tables(pos, half):
    inv = ROPE_THETA ** (-jnp.arange(half, dtype=F32) / half)
    ang = pos.astype(F32)[:, None] * inv[None, :]
    cos, sin = jnp.cos(ang), jnp.sin(ang)
    return jnp.concatenate([cos, cos], -1), jnp.concatenate([-sin, sin], -1)


def _retention_tables(chunk):
    log_g = jnp.log1p(-jnp.exp2(-5.0 - jnp.arange(RET_H, dtype=F32)))
    t = jnp.arange(chunk, dtype=F32)
    diff = t[:, None] - t[None, :]
    decay = jnp.where(diff >= 0, jnp.exp(log_g[:, None, None] * jnp.maximum(diff, 0.0)), 0.0)
    xi = jnp.exp(log_g[:, None] * (t + 1.0))
    zeta = jnp.exp(log_g[:, None] * (chunk - 1.0 - t))
    g_chunk = jnp.broadcast_to(jnp.exp(log_g * chunk)[:, None], (RET_H, chunk))
    cols = jnp.stack([xi, zeta, g_chunk], axis=-1)
    return decay, jnp.pad(cols, ((0, 0), (0, 0), (0, LANES - 3)))


def _head_norm_gate(o, gate):
    mu = jnp.mean(o, -1, keepdims=True)
    oc = o - mu
    var = jnp.mean(oc * oc, -1, keepdims=True)
    return oc * lax.rsqrt(var + LN_EPS) * (gate * jax.nn.sigmoid(gate))


def _rope_rows(x, cos, sin):
    return x * cos + pltpu.roll(x, x.shape[-1] // 2, 1) * sin


def _ret_prompt_kernel(q_ref, k_ref, v_ref, g_ref, cos_ref, sin_ref, decay_ref, cols_ref, o_ref, s_ref, *, chunk):
    n_chunks = q_ref.shape[0] // chunk
    s_ref[...] = jnp.zeros_like(s_ref)

    def body(c, carry):
        r = pl.ds(pl.multiple_of(c * chunk, chunk), chunk)
        cos, sin = cos_ref[r, :], sin_ref[r, :]
        for hh in range(RET_H):
            cols = slice(hh * LANES, (hh + 1) * LANES)
            xi, zeta, g_chunk = cols_ref[hh, :, 0:1], cols_ref[hh, :, 1:2], cols_ref[hh, 0:1, 2:3]
            state = s_ref[0, hh]
            q = _rope_rows(q_ref[r, cols], cos, sin)
            k = _rope_rows(k_ref[r, cols], cos, sin) * (RET_DK ** -0.5)
            v = v_ref[r, cols].astype(BF16)
            inner = _dot_nt(q.astype(BF16), k.astype(BF16)) * decay_ref[hh]
            o = (jnp.dot(inner.astype(BF16), v, preferred_element_type=F32)
                 + jnp.dot((q * xi).astype(BF16), state.astype(BF16), preferred_element_type=F32))
            o_ref[r, cols] = _head_norm_gate(o, g_ref[r, cols])
            kz = (k * zeta).astype(BF16)
            s_ref[0, hh] = g_chunk * state + lax.dot_general(kz, v, (((0,), (0,)), ((), ())),
                                                             preferred_element_type=F32)
        return carry

    lax.fori_loop(0, n_chunks, body, 0)


def _ret_prompt(h, batch, seq):
    chunk = math.gcd(seq, RET_CHUNK)
    assert chunk == RET_CHUNK and RET_DK == LANES and RET_DV == LANES
    cos, sin = _rope_tables(jnp.arange(seq), RET_DK // 2)
    decay, cols = _retention_tables(chunk)
    width = RET_H * LANES
    col = lambda base: pl.BlockSpec((seq, width), lambda b: (b, base // width))
    const = lambda a: pl.BlockSpec(a.shape, lambda b: (0,) * a.ndim)
    return pl.pallas_call(
        functools.partial(_ret_prompt_kernel, chunk=chunk), grid=(batch,),
        in_specs=[col(AB_QR), col(AB_KR), col(AB_VR), col(AB_GR), const(cos), const(sin), const(decay), const(cols)],
        out_specs=[pl.BlockSpec((seq, width), lambda b: (b, 0)),
                   pl.BlockSpec((1, RET_H, RET_DK, RET_DV), lambda b: (b, 0, 0, 0))],
        out_shape=[jax.ShapeDtypeStruct((batch * seq, RET_H * RET_DV), F32),
                   jax.ShapeDtypeStruct((batch, RET_H, RET_DK, RET_DV), F32)],
        compiler_params=_cparams(1), name="ret_prompt")(h, h, h, h, cos, sin, decay, cols)


def _dev_mixer_ab_prompt(x, w_in, w_out, w_cmp1, w_cmp2, cmp_pe):
    batch, seq, d = x.shape
    h = _mm(x.reshape(batch * seq, d), _prep_w_in_ab(w_in)[0])
    w1bd, w2bd, pe_l = _prep_compress(w_cmp1, w_cmp2, cmp_pe, precise=False)
    kvc = _compress_prompt(h, batch, seq, w1bd, w2bd, pe_l)
    o_nsa = _nsa_prompt(h, kvc, batch, seq)
    o_ret, s_new = _ret_prompt(h, batch, seq)
    mixed = jnp.concatenate([o_nsa, o_ret], axis=1)
    mix = _mm(mixed, w_out.astype(BF16))
    new_kv = h[:, AB_KV:AB_KV + 512].reshape(batch, seq, 4, NSA_G, NSA_HD)
    n_keep = min(NSA_WINDOW, seq)
    win = h[:, AB_KV + 512:AB_KV + 768].reshape(batch, seq, 2, NSA_G, NSA_HD)[:, seq - n_keep:]
    return mix.reshape(batch, seq, d), new_kv, win, s_new


def _page_copy(cache_ref, page, col0, width, dst_ref, sem):
    return pltpu.make_async_copy(cache_ref.at[page, :, pl.ds(col0, width)], dst_ref, sem)


def _topn_ids(imp, n_blocks, n_top, out_lane, lane0):
    lane = lax.broadcasted_iota(I32, imp.shape, 1)
    taken = lane >= n_blocks
    ids = jnp.zeros(out_lane.shape, I32)
    for t in range(n_top):
        m = jnp.max(jnp.where(taken, -jnp.inf, imp), axis=1, keepdims=True)
        idx = jnp.min(jnp.where(jnp.logical_not(taken) & (imp >= m), lane.astype(F32), float(imp.shape[1])),
                      axis=1, keepdims=True).astype(I32)
        ids = jnp.where(out_lane == lane0 + t, idx, ids)
        taken = taken | (lane == idx)
    return ids


def _nsa_sample_cmp_kernel(tbl_ref, cache_ref, q_ref, pe_ref, w1_ref, w2_ref, selmap_ref,
                           ocmp_ref, ids_ref,
                           buf, rows, sem, *, n_pages, past):
    b = pl.program_id(0)
    nb = pl.num_programs(0)
    slot = b % 2

    def page(bb, p, sl):
        return pltpu.make_async_copy(cache_ref.at[tbl_ref[bb * n_pages + p], pl.ds(0, 2 * LANES)], buf.at[sl, p],
                                     sem.at[sl])

    def fetch(bb, sl):
        def body(p, c):
            page(bb, p, sl).start()
            return c
        lax.fori_loop(0, n_pages, body, 0)

    @pl.when(b == 0)
    def _():
        fetch(b, slot)

    @pl.when(b + 1 < nb)
    def _():
        fetch(b + 1, 1 - slot)

    def wait(p, c):
        page(b, 0, slot).wait()
        return c
    lax.fori_loop(0, n_pages, wait, 0)

    def to_token_major(p, c):
        r = pl.ds(pl.multiple_of(p * PAGE, PAGE), PAGE)
        rows[0, r, :] = buf[slot, p, 0:LANES, :].T
        rows[1, r, :] = buf[slot, p, LANES:2 * LANES, :].T
        return c
    lax.fori_loop(0, n_pages, to_token_major, 0, unroll=8)

    n_chunks = past // CMP_STRIDE
    n_cmp = n_chunks - CMP_BLOCK // CMP_STRIDE + 1
    n_slc = past // SEL_BLOCK + 1
    k_cmp = _compress_rows(rows.at[0], 0, n_chunks, pe_ref, w1_ref, w2_ref, precise=True)
    v_cmp = _compress_rows(rows.at[1], 1, n_chunks, pe_ref, w1_ref, w2_ref, precise=True)
    q = q_ref[0] * (NSA_HD ** -0.5)
    cmp_idx = lax.broadcasted_iota(I32, (1, n_chunks), 1)
    cmp_mask = (cmp_idx < n_cmp) & (cmp_idx * CMP_STRIDE + (CMP_BLOCK - 1) <= past)
    p = _masked_softmax(_dot_split2(q, k_cmp, nt=True), cmp_mask)
    ocmp_ref[0] = _dot_split2(p, v_cmp)
    p_hi, p_lo = _split_bf16(p)
    imp_h = (jnp.dot(p_hi, selmap_ref[...], preferred_element_type=F32)
             + jnp.dot(p_lo, selmap_ref[...], preferred_element_type=F32))
    lane = lax.broadcasted_iota(I32, (1, imp_h.shape[1]), 1)
    cur = past // SEL_BLOCK
    forced = (lane == 0) | ((lane <= cur) & (lane > cur - SEL_LOCAL))
    out_lane = lax.broadcasted_iota(I32, (1, LANES), 1)
    ids = jnp.zeros((1, LANES), I32)
    for g in range(NSA_G):
        imp = jnp.sum(imp_h[g * NSA_HPG:(g + 1) * NSA_HPG], axis=0, keepdims=True)
        imp = jnp.where(forced, FORCE_SCORE, imp)
        imp = jnp.where(lane * SEL_BLOCK <= past, imp, -jnp.inf)
        ids = ids + _topn_ids(imp, n_slc, min(SEL_TOPN, n_slc), out_lane, g * SEL_TOPN)
    ids_ref[0] = ids


def _nsa_sample_cmp(table, cache, q128, w1bd, w2bd, pe_l, past):
    batch, n_pages = table.shape
    n_chunks = past // CMP_STRIDE
    n_slc = past // SEL_BLOCK + 1
    n_slc_pad = -(-n_slc // LANES) * LANES
    ci = np.arange(n_chunks)[:, None] * CMP_STRIDE
    sj = np.arange(n_slc_pad)[None, :] * SEL_BLOCK
    selmap = np.clip(np.minimum(ci + CMP_BLOCK, sj + SEL_BLOCK) - np.maximum(ci, sj), 0, None) / CMP_STRIDE
    selmap[n_chunks - 1:, :] = 0.0
    selmap[:, n_slc:] = 0.0
    selmap = jnp.asarray(selmap, BF16)
    const = lambda a: pl.BlockSpec(a.shape, lambda b, t: (0,) * a.ndim)
    return pl.pallas_call(
        functools.partial(_nsa_sample_cmp_kernel, n_pages=n_pages, past=past),
        grid_spec=pltpu.PrefetchScalarGridSpec(
            num_scalar_prefetch=1, grid=(batch,),
            in_specs=[pl.BlockSpec(memory_space=pl.ANY), pl.BlockSpec((1, NSA_H, LANES), lambda b, t: (b, 0, 0)),
                      const(pe_l), const(w1bd), const(w2bd), const(selmap)],
            out_specs=[pl.BlockSpec((1, NSA_H, LANES), lambda b, t: (b, 0, 0)),
                       pl.BlockSpec((1, 1, LANES), lambda b, t: (b, 0, 0))],
            scratch_shapes=[pltpu.VMEM((2, n_pages, 2 * LANES, PAGE), F32), pltpu.VMEM((2, past, LANES), F32),
                            pltpu.SemaphoreType.DMA((2,))]),
        out_shape=[jax.ShapeDtypeStruct((batch, NSA_H, LANES), F32), jax.ShapeDtypeStruct((batch, 1, LANES), I32)],
        compiler_params=_cparams(1), name="nsa_sample_cmp")(
            table.reshape(-1), cache, q128, pe_l, w1bd, w2bd, selmap)


def _nsa_sample_attend_kernel(tbl_ref, ids_ref, cache_ref, q_ref, ocmp_ref, gate_ref, new_ref, win_ref,
                              o_ref, wout_ref, sbuf, sem, *, n_pages, past):
    b = pl.program_id(0)
    nb = pl.num_programs(0)
    slot = b % 2
    n_sel = NSA_G * SEL_TOPN
    last_blk = past // SEL_BLOCK
    half = SEL_BLOCK
    per_page = PAGE // SEL_BLOCK

    def block_copy(bb, i, sl):
        blk = jnp.minimum(ids_ref[bb * n_sel + i], last_blk - 1)
        page = tbl_ref[bb * n_pages + blk // per_page]
        return pltpu.make_async_copy(cache_ref.at[page, pl.ds(2 * LANES, 2 * LANES)], sbuf.at[sl, i], sem.at[sl])

    def fetch(bb, sl):
        for i in range(n_sel):
            block_copy(bb, i, sl).start()

    @pl.when(b == 0)
    def _():
        fetch(b, slot)

    @pl.when(b + 1 < nb)
    def _():
        fetch(b + 1, 1 - slot)

    n_win = win_ref.shape[2]
    for i in range(n_sel):
        block_copy(b, i, slot).wait()

    q = (q_ref[0] * (NSA_HD ** -0.5))
    new = new_ref[0]
    gates = jax.nn.sigmoid(gate_ref[0])
    row = lax.broadcasted_iota(I32, (NSA_H, 1), 0)

    def attend(s, mask, v, s_new, v_new, new_ok):
        s = jnp.where(mask, s, -jnp.inf)
        s_new = jnp.where(new_ok, s_new, -jnp.inf)
        m = jnp.maximum(jnp.max(s, -1, keepdims=True), s_new)
        m = jnp.where(m == -jnp.inf, 0.0, m)
        e, e_new = jnp.exp(s - m), jnp.exp(s_new - m)
        den = jnp.maximum(jnp.sum(e, -1, keepdims=True) + e_new, jnp.finfo(F32).tiny)
        return (_dot_split2(e, v) + e_new * v_new) * (1.0 / den)

    o_slc = jnp.zeros((NSA_H, LANES), F32)
    key_lane = lax.broadcasted_iota(I32, (1, SEL_TOPN * PAGE), 1)
    s_new = jnp.sum(q * new[:, 0:LANES], axis=-1, keepdims=True)
    for g in range(NSA_G):
        kv = jnp.concatenate([sbuf[slot, g * SEL_TOPN + i].T for i in range(SEL_TOPN)], axis=0)
        ok = jnp.zeros((1, SEL_TOPN * PAGE), jnp.bool_)
        has_new = False
        for i in range(SEL_TOPN):
            blk = ids_ref[b * n_sel + g * SEL_TOPN + i]
            in_block = (key_lane // PAGE == i) & ((key_lane % PAGE) // half == blk % per_page)
            ok = ok | (in_block & (blk != last_blk))
            has_new = has_new | (blk == last_blk)
        o = attend(_dot_split2(q, kv[:, 0:LANES], nt=True), ok, kv[:, LANES:2 * LANES],
                   s_new, new[:, LANES:2 * LANES], has_new)
        o_slc = jnp.where(row // NSA_HPG == g, o, o_slc)

    win = win_ref[0].T
    wpos = past - n_win + lax.broadcasted_iota(I32, (1, n_win), 1)
    wmask = (wpos >= 0) & (wpos > past - NSA_WINDOW)
    sw_new = jnp.sum(q * new[:, 2 * LANES:3 * LANES], axis=-1, keepdims=True)
    o_win = attend(_dot_split2(q, win[:, 0:LANES], nt=True), wmask, win[:, LANES:2 * LANES],
                   sw_new, new[:, 3 * LANES:4 * LANES], True)

    o = gates[:, 0:1] * ocmp_ref[0] + gates[:, 1:2] * o_slc + gates[:, 2:3] * o_win
    o = jnp.where(row // NSA_HPG == 0, o, pltpu.roll(o, NSA_HD, 1))
    o_ref[0] = o[:, 0:NSA_HD]
    win_row = lax.broadcasted_iota(I32, (n_win, 1), 0)
    wout_ref[0] = jnp.where(win_row == n_win - 1, new[:, 2 * LANES:4 * LANES], pltpu.roll(win, n_win - 1, 0))


def _nsa_sample_attend(table, ids, cache, q128, o_cmp, gates, new, win, past):
    batch, n_pages = table.shape
    n_win = win.shape[2]
    assert n_win == NSA_WINDOW and past >= NSA_WINDOW
    per_b = lambda shape: pl.BlockSpec((1,) + shape, lambda b, t, i: (b, 0, 0))
    return pl.pallas_call(
        functools.partial(_nsa_sample_attend_kernel, n_pages=n_pages, past=past),
        grid_spec=pltpu.PrefetchScalarGridSpec(
            num_scalar_prefetch=2, grid=(batch,),
            in_specs=[pl.BlockSpec(memory_space=pl.ANY), per_b((NSA_H, LANES)), per_b((NSA_H, LANES)),
                      per_b((NSA_H, LANES)), per_b((1, 512)), per_b((256, n_win))],
            out_specs=[per_b((NSA_H, NSA_HD)), per_b((n_win, 256))],
            scratch_shapes=[pltpu.VMEM((2, NSA_G * SEL_TOPN, 2 * LANES, PAGE), F32), pltpu.SemaphoreType.DMA((2,))]),
        out_shape=[jax.ShapeDtypeStruct((batch, NSA_H, NSA_HD), F32), jax.ShapeDtypeStruct((batch, n_win, 256), F32)],
        compiler_params=_cparams(1), name="nsa_sample_attend")(
            table.reshape(-1), ids, cache, q128, o_cmp, gates, new, win)


def _ret_sample_kernel(q_ref, k_ref, v_ref, g_ref, cos_ref, sin_ref, gam_ref, s0_ref, o_ref, s_ref):
    cos, sin = cos_ref[...], sin_ref[...]
    q = _rope_rows(q_ref[0], cos, sin)
    k = _rope_rows(k_ref[0], cos, sin) * (RET_DK ** -0.5)
    v = v_ref[0]
    gam = gam_ref[...]
    inner = jnp.sum(q * k, axis=-1, keepdims=True)
    eye = lax.broadcasted_iota(I32, (RET_DK, RET_DK), 0) == lax.broadcasted_iota(I32, (RET_DK, RET_DK), 1)
    qx = q * gam
    rows = []
    for h in range(RET_H):
        s0 = s0_ref[0, h]
        rows.append(_dot_split(qx, *_split_bf16(s0))[h:h + 1])
        k_col = jnp.sum(jnp.where(eye, k[h:h + 1], 0.0), axis=1, keepdims=True)
        s_ref[0, h] = gam[h:h + 1, 0:1] * s0 + k_col * v[h:h + 1]
    o = inner * v + jnp.concatenate(rows, axis=0)
    o_ref[0] = _head_norm_gate(o, g_ref[0])


def _ret_sample(q, k, v, g, s0, past):
    batch = q.shape[0]
    cos, sin = _rope_tables(jnp.full((1,), past), RET_DK // 2)
    gam = jnp.exp(jnp.log1p(-jnp.exp2(-5.0 - jnp.arange(RET_H, dtype=F32))))[:, None] * jnp.ones((1, LANES), F32)
    row = pl.BlockSpec((1, RET_H, LANES), lambda b: (b, 0, 0))
    const = lambda a: pl.BlockSpec(a.shape, lambda b: (0,) * a.ndim)
    state = pl.BlockSpec((1, RET_H, RET_DK, RET_DV), lambda b: (b, 0, 0, 0))
    return pl.pallas_call(
        _ret_sample_kernel, grid=(batch,),
        in_specs=[row, row, row, row, const(cos), const(sin), const(gam), state],
        out_specs=[row, state],
        out_shape=[jax.ShapeDtypeStruct((batch, RET_H, RET_DV), F32), jax.ShapeDtypeStruct(s0.shape, F32)],
        compiler_params=_cparams(1), name="ret_sample")(q, k, v, g, cos, sin, gam, s0)


def _group_lanes(q):
    g = (jnp.arange(NSA_H) // NSA_HPG)[None, :, None]
    z = jnp.zeros_like(q)
    return jnp.concatenate([jnp.where(g == 0, q, z), jnp.where(g == 1, q, z)], axis=-1)


def _mixer_ab_sample(hs, cache, win, s0, table, w1bd, w2bd, pe_l):
    batch = hs.shape[0]
    past = table.shape[1] * PAGE
    cache = jnp.transpose(cache, (0, 2, 3, 4, 1)).reshape(cache.shape[0], 4 * NSA_G * NSA_HD, PAGE)
    win_t = jnp.transpose(win, (0, 2, 3, 4, 1)).reshape(batch, 2 * NSA_G * NSA_HD, win.shape[1])
    q128 = _group_lanes(hs[:, AB_Q:AB_Q + 512].reshape(batch, NSA_H, NSA_HD))
    o_cmp, ids = _nsa_sample_cmp(table, cache, q128, w1bd, w2bd, pe_l, past)
    gates = jnp.pad(hs[:, AB_GATE:AB_GATE + 24].reshape(batch, NSA_H, 3), ((0, 0), (0, 0), (0, LANES - 3)))
    new = hs[:, AB_KV + 256:AB_KV + 768].reshape(batch, 1, 512)
    o_nsa, win_new = _nsa_sample_attend(table, ids[:, 0, :NSA_G * SEL_TOPN].reshape(-1), cache, q128, o_cmp, gates,
                                        new, win_t, past)
    seg = lambda c: hs[:, c:c + 512].reshape(batch, RET_H, RET_DK)
    o_ret, s_new = _ret_sample(seg(AB_QR), seg(AB_KR), seg(AB_VR), seg(AB_GR), s0, past)
    mixed = jnp.concatenate([o_nsa.reshape(batch, 512), o_ret.reshape(batch, 512)], axis=1)
    return mixed, win_new.reshape(win.shape), s_new


def _dev_mixer_ab_sample(x, cache, win, s0, table, w_in, w_out, w_cmp1, w_cmp2, cmp_pe):
    batch = x.shape[0]
    xs = jnp.pad(x.reshape(batch, -1), ((0, TOK_TILE - batch), (0, 0)))
    hs = _mm(xs, *_prep_w_in_ab(w_in), precise_from=0)[:batch]
    mixed, win_new, s_new = _mixer_ab_sample(hs, cache, win, s0, table,
                                             *_prep_compress(w_cmp1, w_cmp2, cmp_pe, precise=True))
    mix = _mm(jnp.pad(mixed, ((0, TOK_TILE - batch), (0, 0))), *_split_bf16(w_out), precise_from=0)[:batch]
    new_kv = hs[:, AB_KV:AB_KV + 512].reshape(batch, 1, 4, NSA_G, NSA_HD)
    return mix.reshape(batch, 1, -1), new_kv, win_new, s_new


MLA_ROW_W = 384
MLA_QK = LANES
MLA_CW = 1024
MLA_KR, MLA_KR_ROT = 640, 768


def _rot_half_cols(w):
    half = w.shape[-1] // 2
    return jnp.concatenate([-w[..., half:], w[..., :half]], axis=-1)


def _prep_mla(w_in, w_qb, w_kb, w_vb):
    d = w_in.shape[0]
    kr = w_in[:, MLA_Q_LORA + MLA_KV_LORA:]
    z = lambda n: jnp.zeros((d, n), w_in.dtype)
    wc = jnp.concatenate([w_in[:, :MLA_Q_LORA + MLA_KV_LORA], kr, z(LANES - MLA_ROPE), _rot_half_cols(kr),
                          z(MLA_CW - MLA_KR_ROT - MLA_ROPE)], axis=1)
    zq = jnp.zeros((MLA_Q_LORA, MLA_H, MLA_QK - MLA_NOPE - MLA_ROPE), w_qb.dtype)
    wq = jnp.concatenate([w_qb, zq], axis=-1).reshape(MLA_Q_LORA, MLA_H * MLA_QK)
    wq_rot = jnp.concatenate([jnp.zeros_like(w_qb[..., :MLA_NOPE]), _rot_half_cols(w_qb[..., MLA_NOPE:]), zq],
                             axis=-1).reshape(MLA_Q_LORA, MLA_H * MLA_QK)
    eye = jnp.eye(MLA_ROPE, dtype=w_kb.dtype)
    wk = jnp.zeros((MLA_ROW_W, MLA_H, MLA_QK), w_kb.dtype)
    wk = wk.at[:MLA_KV_LORA, :, :MLA_NOPE].set(w_kb)
    wk = wk.at[MLA_KV_LORA:MLA_KV_LORA + MLA_ROPE, :, MLA_NOPE:MLA_NOPE + MLA_ROPE].set(
        jnp.broadcast_to(eye[:, None, :], (MLA_ROPE, MLA_H, MLA_ROPE)))
    wv = jnp.zeros((MLA_ROW_W, MLA_H, MLA_V), w_vb.dtype).at[:MLA_KV_LORA].set(w_vb)
    wkv = jnp.concatenate([wk.reshape(MLA_ROW_W, -1), wv.reshape(MLA_ROW_W, -1)], axis=1)
    wabs = jnp.zeros((MLA_H, MLA_QK, MLA_ROW_W), w_kb.dtype)
    wabs = wabs.at[:, :MLA_NOPE, :MLA_KV_LORA].set(jnp.transpose(w_kb, (1, 2, 0)))
    wabs = wabs.at[:, MLA_NOPE:MLA_NOPE + MLA_ROPE, MLA_KV_LORA:MLA_KV_LORA + MLA_ROPE].set(
        jnp.broadcast_to(eye[None], (MLA_H, MLA_ROPE, MLA_ROPE)))
    wvt = jnp.transpose(w_vb, (1, 0, 2))
    return (wc.astype(BF16), wq.astype(BF16), wq_rot.astype(BF16), wkv.astype(BF16), wabs.astype(BF16),
            wvt.astype(BF16))


def _mla_rope_tables(pos):
    half = MLA_ROPE // 2
    inv = ROPE_THETA ** (-jnp.arange(half, dtype=F32) / half)
    ang = pos.astype(F32)[:, None] * inv[None, :]
    cos, sin = jnp.cos(ang), jnp.sin(ang)
    n = pos.shape[0]
    cos2, sin2 = jnp.concatenate([cos, cos], -1), jnp.concatenate([sin, sin], -1)
    z = lambda w: jnp.zeros((n, w), F32)
    cos_k = jnp.concatenate([cos2, z(LANES - MLA_ROPE)], -1)
    sin_k = jnp.concatenate([sin2, z(LANES - MLA_ROPE)], -1)
    cos_q = jnp.concatenate([jnp.ones((n, MLA_NOPE), F32), cos2, z(MLA_QK - MLA_NOPE - MLA_ROPE)], -1)
    sin_q = jnp.concatenate([z(MLA_NOPE), sin2, z(MLA_QK - MLA_NOPE - MLA_ROPE)], -1)
    return cos_k, sin_k, cos_q, sin_q


def _rms_rows(x, g):
    return x * lax.rsqrt(jnp.mean(x * x, -1, keepdims=True) + RMS_EPS) * g


def _mla_in_kernel(x_ref, w_ref, gq_ref, gkv_ref, cos_ref, sin_ref, cq_ref, rows_ref):
    xb = x_ref[...].astype(BF16)
    h = jnp.concatenate([jnp.dot(xb, w_ref[:, c:c + 2 * MXU_N], preferred_element_type=F32)
                         for c in range(0, MLA_CW, 2 * MXU_N)], axis=1)
    cq_ref[...] = _rms_rows(h[:, :MLA_Q_LORA], gq_ref[...]).astype(cq_ref.dtype)
    rows_ref[:, :MLA_KV_LORA] = _rms_rows(h[:, MLA_Q_LORA:MLA_Q_LORA + MLA_KV_LORA], gkv_ref[...])
    rows_ref[:, MLA_KV_LORA:] = (h[:, MLA_KR:MLA_KR + LANES] * cos_ref[...]
                                 + h[:, MLA_KR_ROT:MLA_KR_ROT + LANES] * sin_ref[...])


def _pos_block(n_prompt_tiles, tiles_per_seq):
    return lambda i: (jnp.where(i < n_prompt_tiles, i % tiles_per_seq, tiles_per_seq + i - n_prompt_tiles), 0)


def _mla_in(x, wc, gq, gkv, cos_k, sin_k, n_prompt_tiles, tiles_per_seq, tile=TOK_TILE):
    n, d = x.shape
    pos = pl.BlockSpec((tile, LANES), _pos_block(n_prompt_tiles, tiles_per_seq))
    vec = lambda w: pl.BlockSpec((1, w), lambda i: (0, 0))
    return pl.pallas_call(
        _mla_in_kernel, grid=(n // tile,),
        in_specs=[pl.BlockSpec((tile, d), lambda i: (i, 0)), pl.BlockSpec((d, MLA_CW), lambda i: (0, 0)),
                  vec(MLA_Q_LORA), vec(MLA_KV_LORA), pos, pos],
        out_specs=[pl.BlockSpec((tile, MLA_Q_LORA), lambda i: (i, 0)), pl.BlockSpec((tile, MLA_ROW_W), lambda i: (i, 0))],
        out_shape=[jax.ShapeDtypeStruct((n, MLA_Q_LORA), BF16), jax.ShapeDtypeStruct((n, MLA_ROW_W), F32)],
        compiler_params=_cparams(1), name="mla_in")(
            x, wc, gq.reshape(1, -1), gkv.reshape(1, -1), cos_k, sin_k)


def _mla_q_kernel(c_ref, w_ref, wrot_ref, cos_ref, sin_ref, q_ref):
    c = c_ref[...]
    cos, sin = cos_ref[...], sin_ref[...]
    for h in range(MLA_H):
        cols = slice(h * MLA_QK, (h + 1) * MLA_QK)
        q = jnp.dot(c, w_ref[:, cols], preferred_element_type=F32)
        qr = jnp.dot(c, wrot_ref[:, cols], preferred_element_type=F32)
        q_ref[:, cols] = (q * cos + qr * sin).astype(q_ref.dtype)


def _mla_q(cq, wq, wq_rot, cos_q, sin_q, n_prompt_tiles, tiles_per_seq, tile=TOK_TILE):
    n, k = cq.shape
    m = wq.shape[1]
    pos = pl.BlockSpec((tile, LANES), _pos_block(n_prompt_tiles, tiles_per_seq))
    w = pl.BlockSpec((k, m), lambda i: (0, 0))
    return pl.pallas_call(
        _mla_q_kernel, grid=(n // tile,),
        in_specs=[pl.BlockSpec((tile, k), lambda i: (i, 0)), w, w, pos, pos],
        out_specs=pl.BlockSpec((tile, m), lambda i: (i, 0)),
        out_shape=jax.ShapeDtypeStruct((n, m), BF16),
        compiler_params=_cparams(1), name="mla_q")(cq, wq, wq_rot, cos_q, sin_q)


MLA_ATT_TILE = 512
MLA_ATT_CHUNK = 256


def _mla_prompt_kernel(q_ref, k_ref, v_ref, o_ref):
    ck = MLA_ATT_CHUNK
    n_sub = q_ref.shape[0] // ck
    qi = pl.program_id(2)
    c2 = (MLA_NOPE + MLA_ROPE) ** -0.5 * math.log2(math.e)
    lane = lax.broadcasted_iota(I32, (ck, LANES), 1)
    diag = lax.broadcasted_iota(I32, (ck, ck), 1) <= lax.broadcasted_iota(I32, (ck, ck), 0)
    problems = [(hh, r) for r in range(n_sub) for hh in range(2)]
    q = {(hh, r): q_ref[r * ck:(r + 1) * ck, hh * MLA_QK:(hh + 1) * MLA_QK] for hh, r in problems}

    def step(prob, kc, carry, masked):
        hh, _ = prob
        m, l, acc = carry
        rows = pl.ds(pl.multiple_of(kc * ck, ck), ck)
        s = _dot_nt(q[prob], k_ref[rows, hh * MLA_QK:(hh + 1) * MLA_QK])
        if masked:
            s = jnp.where(diag, s, -jnp.inf)
        m_new = jnp.maximum(m, jnp.max(s, -1, keepdims=True))
        a = jnp.exp2((m - m_new) * c2)
        e = jnp.exp2((s - m_new) * c2)
        l = a * l + jnp.sum(e, -1, keepdims=True)
        acc = a * acc + jnp.dot(e.astype(BF16), v_ref[rows, :], preferred_element_type=F32)
        return m_new, l, acc

    init = tuple((jnp.full((ck, 1), -jnp.inf, F32), jnp.zeros((ck, 1), F32), jnp.zeros((ck, LANES), F32))
                 for _ in problems)

    def full_chunks(kc, carries):
        return tuple(step(p, kc, c, False) for p, c in zip(problems, carries))

    carries = list(lax.fori_loop(0, n_sub * qi, full_chunks, init))
    for idx, prob in enumerate(problems):
        r = prob[1]
        for c in range(r + 1):
            carries[idx] = step(prob, n_sub * qi + c, carries[idx], c == r)
    for r in range(n_sub):
        outs = []
        for hh in range(2):
            m, l, acc = carries[problems.index((hh, r))]
            outs.append(acc * (1.0 / l))
        o_ref[r * ck:(r + 1) * ck, :] = jnp.where(lane < MLA_V, outs[0], outs[1]).astype(o_ref.dtype)


def _mla_prompt(q, kv, batch, seq):
    tq = min(MLA_ATT_TILE, seq)
    assert tq % MLA_ATT_CHUNK == 0
    nq = seq // tq
    k_cols = MLA_H * MLA_QK // (2 * MLA_QK)
    return pl.pallas_call(
        _mla_prompt_kernel, grid=(batch, MLA_H // 2, nq),
        in_specs=[pl.BlockSpec((tq, 2 * MLA_QK), lambda b, j, i: (b * nq + i, j)),
                  pl.BlockSpec((seq, 2 * MLA_QK), lambda b, j, i: (b, j)),
                  pl.BlockSpec((seq, 2 * MLA_V), lambda b, j, i: (b, 2 * k_cols + j))],
        out_specs=pl.BlockSpec((tq, 2 * MLA_V), lambda b, j, i: (b * nq + i, j)),
        out_shape=jax.ShapeDtypeStruct((batch * seq, MLA_H * MLA_V), BF16),
        compiler_params=_cparams(3), name="mla_prompt")(q, kv, kv)


def _mla_absorb_kernel(q_ref, w_ref, o_ref):
    o_ref[0] = jnp.dot(q_ref[...], w_ref[0], preferred_element_type=F32)


def _mla_absorb(qs, wabs):
    batch = qs.shape[0]
    return pl.pallas_call(
        _mla_absorb_kernel, grid=(MLA_H,),
        in_specs=[pl.BlockSpec((batch, MLA_QK), lambda h: (0, h)), pl.BlockSpec((1, MLA_QK, MLA_ROW_W), lambda h: (h, 0, 0))],
        out_specs=pl.BlockSpec((1, batch, MLA_ROW_W), lambda h: (h, 0, 0)),
        out_shape=jax.ShapeDtypeStruct((MLA_H, batch, MLA_ROW_W), F32),
        compiler_params=_cparams(1), name="mla_absorb")(qs, wabs)


MLA_PAGE_GROUP = 16


def _mla_sample_kernel(tbl_ref, cache_ref, q_ref, new_ref, o_ref, buf, pbuf, s_scr, e_scr, sem, *, n_pages):
    b = pl.program_id(0)
    nb = pl.num_programs(0)
    slot = b % 2
    width = MLA_KV_LORA + MLA_ROPE

    def page(bb, p, sl):
        return pltpu.make_async_copy(cache_ref.at[tbl_ref[bb * n_pages + p]], buf.at[sl, p], sem.at[sl])

    def fetch(bb, sl):
        def body(p, c):
            page(bb, p, sl).start()
            return c
        lax.fori_loop(0, n_pages, body, 0)

    @pl.when(b == 0)
    def _():
        fetch(b, slot)

    @pl.when(b + 1 < nb)
    def _():
        fetch(b + 1, 1 - slot)

    def wait(p, c):
        page(b, 0, slot).wait()
        return c
    lax.fori_loop(0, n_pages, wait, 0)

    scale = (MLA_NOPE + MLA_ROPE) ** -0.5
    q = q_ref[0]
    qb = q[:, :width].astype(BF16)

    group = math.gcd(n_pages, MLA_PAGE_GROUP)

    def score(c, carry):
        pages = [buf[slot, c * group + i].astype(BF16) for i in range(group)]
        for i in range(group):
            pbuf[c * group + i] = pages[i]
        s = jnp.dot(qb, jnp.concatenate(pages, axis=1), preferred_element_type=F32)
        for i in range(group):
            s_scr[c * group + i] = s[:, i * PAGE:(i + 1) * PAGE]
        return carry
    lax.fori_loop(0, n_pages // group, score, 0)

    s = s_scr[...] * scale
    new = new_ref[0]
    s_new = jnp.sum(q * new, axis=-1, keepdims=True) * scale
    m = jnp.maximum(jnp.max(jnp.max(s, axis=0), -1, keepdims=True), s_new)
    e, e_new = jnp.exp(s - m), jnp.exp(s_new - m)
    den = jnp.sum(jnp.sum(e, axis=0), -1, keepdims=True) + e_new
    e_scr[...] = e.astype(BF16)

    def weighted(c, acc):
        probs = jnp.concatenate([e_scr[c * group + i] for i in range(group)], axis=1)
        latents = jnp.concatenate([pbuf[c * group + i, 0:MLA_KV_LORA, :] for i in range(group)], axis=1)
        return acc + _dot_nt(probs, latents)
    o = lax.fori_loop(0, n_pages // group, weighted, jnp.zeros((MLA_H, MLA_KV_LORA), F32))
    o_ref[0] = (o + e_new * new[:, :MLA_KV_LORA]) * (1.0 / den)


def _mla_sample(table, cache, qabs, new_rows):
    batch, n_pages = table.shape
    width = cache.shape[1]
    return pl.pallas_call(
        functools.partial(_mla_sample_kernel, n_pages=n_pages),
        grid_spec=pltpu.PrefetchScalarGridSpec(
            num_scalar_prefetch=1, grid=(batch,),
            in_specs=[pl.BlockSpec(memory_space=pl.ANY), pl.BlockSpec((1, MLA_H, MLA_ROW_W), lambda b, t: (b, 0, 0)),
                      pl.BlockSpec((1, 1, MLA_ROW_W), lambda b, t: (b, 0, 0))],
            out_specs=pl.BlockSpec((1, MLA_H, MLA_KV_LORA), lambda b, t: (b, 0, 0)),
            scratch_shapes=[pltpu.VMEM((2, n_pages, width, PAGE), F32), pltpu.VMEM((n_pages, width, PAGE), BF16),
                            pltpu.VMEM((n_pages, MLA_H, PAGE), F32),
                            pltpu.VMEM((n_pages, MLA_H, PAGE), BF16), pltpu.SemaphoreType.DMA((2,))]),
        out_shape=jax.ShapeDtypeStruct((batch, MLA_H, MLA_KV_LORA), F32),
        compiler_params=_cparams(1), name="mla_sample")(table.reshape(-1), cache, qabs, new_rows)


def _mla_vup_kernel(o_ref, w_ref, y_ref):
    y_ref[0] = jnp.dot(o_ref[0].astype(BF16), w_ref[0], preferred_element_type=F32)


def _mla_vup(o_lat, wvt):
    _, batch, lat = o_lat.shape
    return pl.pallas_call(
        _mla_vup_kernel, grid=(MLA_H,),
        in_specs=[pl.BlockSpec((1, batch, lat), lambda h: (h, 0, 0)), pl.BlockSpec((1, lat, MLA_V), lambda h: (h, 0, 0))],
        out_specs=pl.BlockSpec((1, batch, MLA_V), lambda h: (h, 0, 0)),
        out_shape=jax.ShapeDtypeStruct((MLA_H, batch, MLA_V), F32),
        compiler_params=_cparams(1), name="mla_vup")(o_lat, wvt)


def _mixer_c(x, n_prompt, batch_p, seq, batch_s, cache, table, w_in, gq, gkv, w_qb, w_kb, w_vb):
    n = x.shape[0]
    past = table.shape[1] * PAGE
    wc, wq, wq_rot, wkv, wabs, wvt = _prep_mla(w_in, w_qb, w_kb, w_vb)
    pos = jnp.concatenate([jnp.arange(seq), jnp.full((n - n_prompt,), past)])
    cos_k, sin_k, cos_q, sin_q = _mla_rope_tables(pos)
    n_pt, tps = n_prompt // TOK_TILE, seq // TOK_TILE
    cq, rows = _mla_in(x, wc, gq, gkv, cos_k, sin_k, n_pt, tps)
    q = _mla_q(cq, wq, wq_rot, cos_q, sin_q, n_pt, tps)
    kv = _mm(rows, wkv, out_dtype=BF16)
    attn_p = _mla_prompt(q, kv, batch_p, seq)
    qabs = _mla_absorb(q[n_prompt:n_prompt + batch_s], wabs)
    new_rows = rows[n_prompt:n_prompt + batch_s].reshape(batch_s, 1, MLA_ROW_W)
    o_lat = _mla_sample(table, jnp.transpose(cache, (0, 2, 1)), jnp.transpose(qabs, (1, 0, 2)), new_rows)
    o_s = _mla_vup(jnp.transpose(o_lat, (1, 0, 2)), wvt)
    o_s = jnp.transpose(o_s, (1, 0, 2)).reshape(batch_s, MLA_H * MLA_V).astype(BF16)
    attn = jnp.concatenate([attn_p, o_s, jnp.zeros((n - n_prompt - batch_s, MLA_H * MLA_V), BF16)], axis=0)
    return attn, rows


def _dev_mixer_c(xp, xs, cache, table, w_in, gq, gkv, w_qb, w_kb, w_vb, w_out):
    bp, seq, d = xp.shape
    bs = xs.shape[0]
    n_prompt = bp * seq
    x = jnp.concatenate([xp.reshape(n_prompt, d), xs.reshape(bs, d), jnp.zeros((TOK_TILE - bs, d), F32)], axis=0)
    attn, rows = _mixer_c(x, n_prompt, bp, seq, bs, cache, table, w_in, gq, gkv, w_qb, w_kb, w_vb)
    mix = _mm(attn, w_out.astype(BF16))
    width = MLA_KV_LORA + MLA_ROPE
    return ((mix[:n_prompt].reshape(bp, seq, d), rows[:n_prompt, :width].reshape(bp, seq, width)),
            (mix[n_prompt:n_prompt + bs].reshape(bs, 1, d), rows[n_prompt:n_prompt + bs, :width].reshape(bs, 1, width)))


def kernel(x_prompt, x_sample, cache_nsa_kv, state_nsa_win, state_ret, cache_mla, page_table, w_in_ab, w_out_ab, w_cmp1, w_cmp2, cmp_pe, w_in_mla, mla_q_norm, mla_kv_norm, w_q_up, w_k_up, w_v_up, w_out_mla, ln_mix_g, ln_mix_b, ln_ffn_g, ln_ffn_b, w_router, router_bias, w_exp_gate, w_exp_up, w_exp_down):
    bp, seq, d = x_prompt.shape
    bs = x_sample.shape[0]
    assert x_sample.shape[1] == 1 and bs <= TOK_TILE and seq % TOK_TILE == 0 and ln_mix_g.shape[0] == DEPTH
    n_prompt = bp * seq
    pad_rows = TOK_TILE - bs
    x = jnp.concatenate([x_prompt.reshape(n_prompt, d), x_sample.reshape(bs, d), jnp.zeros((pad_rows, d), F32)], axis=0)
    sample = slice(n_prompt, n_prompt + bs)
    sample_tile = n_prompt // TOK_TILE
    mla_w = MLA_KV_LORA + MLA_ROPE
    kv_p, kv_s, win_p, win_s, ret_p, ret_s, mla_p, mla_s = [], [], [], [], [], [], [], []
    for layer in range(DEPTH):
        i = layer // 2
        if layer % 2 == 0:
            h = _mm(x, *_prep_w_in_ab(w_in_ab[i]), precise_from=sample_tile)
            kvc = _compress_prompt(h, bp, seq, *_prep_compress(w_cmp1[i], w_cmp2[i], cmp_pe[i], precise=False))
            o_nsa = _nsa_prompt(h, kvc, bp, seq)
            o_ret, s_p = _ret_prompt(h, bp, seq)
            hs = h[sample]
            mixed_s, w_s, s_s = _mixer_ab_sample(hs, cache_nsa_kv[i], state_nsa_win[i], state_ret[i], page_table,
                                                 *_prep_compress(w_cmp1[i], w_cmp2[i], cmp_pe[i], precise=True))
            mixed = jnp.concatenate([jnp.concatenate([o_nsa, o_ret], axis=1), mixed_s,
                                     jnp.zeros((pad_rows, mixed_s.shape[1]), F32)], axis=0)
            w_out, w_out_lo = _split_bf16(w_out_ab[i])
            out_precise = sample_tile
            kv_p.append(h[:n_prompt, AB_KV:AB_KV + 512].reshape(bp, seq, 4, NSA_G, NSA_HD))
            kv_s.append(hs[:, AB_KV:AB_KV + 512].reshape(bs, 1, 4, NSA_G, NSA_HD))
            n_keep = min(NSA_WINDOW, seq)
            win_p.append(h[:n_prompt, AB_KV + 512:AB_KV + 768].reshape(bp, seq, 2, NSA_G, NSA_HD)[:, seq - n_keep:])
            win_s.append(w_s)
            ret_p.append(s_p)
            ret_s.append(s_s)
        else:
            mixed, rows = _mixer_c(x, n_prompt, bp, seq, bs, cache_mla[i], page_table, w_in_mla[i], mla_q_norm[i],
                                   mla_kv_norm[i], w_q_up[i], w_k_up[i], w_v_up[i])
            w_out, w_out_lo, out_precise = w_out_mla[i].astype(BF16), None, None
            mla_p.append(rows[:n_prompt, :mla_w].reshape(bp, seq, mla_w))
            mla_s.append(rows[sample, :mla_w].reshape(bs, 1, mla_w))
        x = _proj_ln(mixed, w_out, x, ln_mix_g[layer], ln_mix_b[layer], w_lo=w_out_lo, precise_from=out_precise)
        x = _moe_ln(x, w_router, router_bias, w_exp_gate[layer].astype(BF16), w_exp_up[layer].astype(BF16),
                    w_exp_down[layer].astype(BF16), ln_ffn_g[layer], ln_ffn_b[layer])
    return (x[:n_prompt].reshape(bp, seq, d), x[sample].reshape(bs, 1, d), jnp.stack(kv_p), jnp.stack(kv_s),
            jnp.stack(win_p), jnp.stack(win_s), jnp.stack(ret_p), jnp.stack(ret_s), jnp.stack(mla_p), jnp.stack(mla_s))
```
